```python
import jax, jax.numpy as jnp
from jax import lax
import numpy as np

D_MODEL = 1024
BATCH = 8
SEQ = 8192
DEPTH = 4

CTX_LEN = 256
GRID_W = 64
D_MIX = D_MODEL
D_RNN = D_MIX // 2
D_ATTN = D_MIX - D_RNN
HEAD_DIM = 64
N_Q_HEADS = D_ATTN // HEAD_DIM
N_KV_HEADS = 2
GQ = N_Q_HEADS // N_KV_HEADS
D_KV = N_KV_HEADS * HEAD_DIM
WINDOW = 128
BLK = 128
ATTN_SCALE = HEAD_DIM ** -0.5
AXIS_DIM = HEAD_DIM // 2
ROPE_BASE = 10000.0
RG_BLOCKS = 8
RG_BD = D_RNN // RG_BLOCKS
RG_C = 8.0
CONV_W = 4
CONV_LEFT = 2
N_GROUPS = 4
EXP_PER_GROUP = 8
N_EXPERTS = N_GROUPS * EXP_PER_GROUP
TOP_K = 2
D_EXPERT = 512
MOE_BLK = 128
D_IN = 2 * D_RNN + D_ATTN + 2 * D_KV
ALPHA = (2 * DEPTH) ** 0.25
BETA = (8 * DEPTH) ** -0.25
LN_EPS = 1e-6
NEG_INF = -1e30

kernel_name = "hybrid_rglru_swa_hmoe_diffusion_trunk"


def layer_norm(x, g, b):
    xf = x.astype(jnp.float32)
    mu = jnp.mean(xf, -1, keepdims=True)
    var = jnp.mean(jnp.square(xf - mu), -1, keepdims=True)
    return ((xf - mu) * lax.rsqrt(var + LN_EPS) * g + b).astype(x.dtype)


def modulate(x, shift, scale):
    xf = x.astype(jnp.float32)
    mu = jnp.mean(xf, -1, keepdims=True)
    var = jnp.mean(jnp.square(xf - mu), -1, keepdims=True)
    return ((xf - mu) * lax.rsqrt(var + LN_EPS)).astype(x.dtype) * (1.0 + scale) + shift


def rms_norm(x, g):
    xf = x.astype(jnp.float32)
    return (xf * lax.rsqrt(jnp.mean(jnp.square(xf), -1, keepdims=True) + LN_EPS) * g).astype(x.dtype)


def axial_rope_tables(seq):
    rows = seq // GRID_W
    row = jnp.repeat(jnp.arange(rows, dtype=jnp.float32), GRID_W)
    col = jnp.tile(jnp.arange(GRID_W, dtype=jnp.float32), rows)
    inv = ROPE_BASE ** (-jnp.arange(0, AXIS_DIM, 2, dtype=jnp.float32) / AXIS_DIM)
    ang_r = row[:, None] * inv
    ang_c = col[:, None] * inv
    return (jnp.cos(ang_r), jnp.sin(ang_r), jnp.cos(ang_c), jnp.sin(ang_c))


def _rotate(x, cos, sin):
    cos = cos[None, :, None, :].astype(x.dtype)
    sin = sin[None, :, None, :].astype(x.dtype)
    x1, x2 = jnp.split(x, 2, axis=-1)
    return jnp.concatenate([x1 * cos - x2 * sin, x2 * cos + x1 * sin], -1)


def rope_axial(x, rope):
    cos_r, sin_r, cos_c, sin_c = rope
    return jnp.concatenate([_rotate(x[..., :AXIS_DIM], cos_r, sin_r),
                            _rotate(x[..., AXIS_DIM:], cos_c, sin_c)], -1)


def centred_dwconv(u, w, b):
    L = u.shape[1]
    up = jnp.pad(u, ((0, 0), (CONV_LEFT, CONV_W - 1 - CONV_LEFT), (0, 0)))
    out = b
    for j in range(CONV_W):
        out = out + up[:, j:j + L] * w[j]
    return out


def _linear_recurrence_combine(e1, e2):
    a1, b1 = e1
    a2, b2 = e2
    return a1 * a2, a2 * b1 + b2


def rglru_scan(x, wa, ba, wx, bx, lam, h0, reverse):
    B, L, C = x.shape
    xh = x.reshape(B, L, RG_BLOCKS, RG_BD)
    r = jax.nn.sigmoid(jnp.einsum('blhi,hij->blhj', xh, wa).reshape(B, L, C) + ba)
    gi = jax.nn.sigmoid(jnp.einsum('blhi,hij->blhj', xh, wx).reshape(B, L, C) + bx)
    log_a = -RG_C * r.astype(jnp.float32) * jax.nn.softplus(-lam.astype(jnp.float32))
    a = jnp.exp(log_a)
    b = jnp.sqrt(-jnp.expm1(2.0 * log_a)) * (gi * x).astype(jnp.float32)
    if reverse:
        a = jnp.flip(a, 1)
        b = jnp.flip(b, 1)
    b = b.at[:, 0].add(a[:, 0] * h0)
    _, h = lax.associative_scan(_linear_recurrence_combine, (a, b), axis=1)
    h_end = h[:, -1]
    if reverse:
        h = jnp.flip(h, 1)
    return h.astype(x.dtype), h_end


def rglru_group(u_lat, u_ctx, conv_w, conv_b, wa, ba, wx, bx, lam, with_ctx_out):
    x_lat = centred_dwconv(u_lat, conv_w, conv_b)
    x_ctx = centred_dwconv(u_ctx, conv_w, conv_b)
    h0 = jnp.zeros((u_ctx.shape[0], D_RNN), jnp.float32)
    y_lat = 0.0
    y_ctx = 0.0
    for d in range(2):
        rev = d == 1
        h_ctx, h_end = rglru_scan(x_ctx, wa[d], ba[d], wx[d], bx[d], lam[d], h0, rev)
        h_lat, _ = rglru_scan(x_lat, wa[d], ba[d], wx[d], bx[d], lam[d], h_end, rev)
        y_lat = y_lat + h_lat
        if with_ctx_out:
            y_ctx = y_ctx + h_ctx
    return y_lat, (y_ctx if with_ctx_out else None)


def window_attention(q, k, v, kc, vc, sink):
    B, S = q.shape[:2]
    nb = S // BLK
    qb = q.reshape(B, nb, BLK, N_KV_HEADS, GQ, HEAD_DIM)
    pad = ((0, 0), (BLK, BLK), (0, 0), (0, 0))
    kp = jnp.pad(k, pad).reshape(B, nb + 2, BLK, N_KV_HEADS, HEAD_DIM)
    vp = jnp.pad(v, pad).reshape(B, nb + 2, BLK, N_KV_HEADS, HEAD_DIM)
    kb = jnp.concatenate([kp[:, :-2], kp[:, 1:-1], kp[:, 2:]], axis=2)
    vb = jnp.concatenate([vp[:, :-2], vp[:, 1:-1], vp[:, 2:]], axis=2)
    s_band = jnp.einsum('bnqhgd,bnkhd->bnhgqk', qb, kb).astype(jnp.float32) * ATTN_SCALE
    offs = jnp.arange(3 * BLK) - BLK
    in_win = jnp.abs(offs[None, :] - jnp.arange(BLK)[:, None]) <= WINDOW
    kpos = jnp.arange(nb)[:, None] * BLK + offs[None, :]
    in_rng = (kpos >= 0) & (kpos < S)
    mask = in_win[None] & in_rng[:, None, :]
    s_band = jnp.where(mask[None, :, None, None], s_band, NEG_INF)
    s_ctx = jnp.einsum('bnqhgd,bchd->bnhgqc', qb, kc).astype(jnp.float32) * ATTN_SCALE
    s_sink = jnp.broadcast_to(sink.astype(jnp.float32).reshape(1, 1, N_KV_HEADS, GQ, 1, 1),
                              s_band.shape[:-1] + (1,))
    p = jax.nn.softmax(jnp.concatenate([s_band, s_ctx, s_sink], -1), axis=-1)
    C = kc.shape[1]
    p_band = p[..., :3 * BLK].astype(v.dtype)
    p_ctx = p[..., 3 * BLK:3 * BLK + C].astype(v.dtype)
    o = (jnp.einsum('bnhgqk,bnkhd->bnqhgd', p_band, vb)
         + jnp.einsum('bnhgqc,bchd->bnqhgd', p_ctx, vc))
    return o.reshape(B, S, D_ATTN)


def context_attention(qc, kc, vc, sink):
    B, C = qc.shape[:2]
    qg = qc.reshape(B, C, N_KV_HEADS, GQ, HEAD_DIM)
    s = jnp.einsum('bqhgd,bkhd->bhgqk', qg, kc).astype(jnp.float32) * ATTN_SCALE
    s_sink = jnp.broadcast_to(sink.astype(jnp.float32).reshape(1, N_KV_HEADS, GQ, 1, 1),
                              s.shape[:-1] + (1,))
    p = jax.nn.softmax(jnp.concatenate([s, s_sink], -1), axis=-1)[..., :C]
    o = jnp.einsum('bhgqk,bkhd->bqhgd', p.astype(vc.dtype), vc)
    return o.reshape(B, C, D_ATTN)


def token_mixer(h_lat, h_ctx, rope, w_in, conv_w, conv_b, rg_wa, rg_ba, rg_wx, rg_bx, rg_lam,
                sink, gn_rnn, gn_attn, w_out, with_ctx_out):
    B, S, _ = h_lat.shape
    C = h_ctx.shape[1]
    cuts = [D_RNN, 2 * D_RNN, 2 * D_RNN + D_ATTN, 2 * D_RNN + D_ATTN + D_KV]
    u_l, g_l, q_l, k_l, v_l = jnp.split(h_lat @ w_in, cuts, axis=-1)
    u_c, g_c, q_c, k_c, v_c = jnp.split(h_ctx @ w_in, cuts, axis=-1)
    r_l, r_c = rglru_group(u_l, u_c, conv_w, conv_b, rg_wa, rg_ba, rg_wx, rg_bx, rg_lam, with_ctx_out)
    q_l = rope_axial(q_l.reshape(B, S, N_Q_HEADS, HEAD_DIM), rope)
    k_l = rope_axial(k_l.reshape(B, S, N_KV_HEADS, HEAD_DIM), rope)
    v_l = v_l.reshape(B, S, N_KV_HEADS, HEAD_DIM)
    k_c = k_c.reshape(B, C, N_KV_HEADS, HEAD_DIM)
    v_c = v_c.reshape(B, C, N_KV_HEADS, HEAD_DIM)
    a_l = window_attention(q_l, k_l, v_l, k_c, v_c, sink)
    o_lat = jnp.concatenate([rms_norm(r_l * jax.nn.gelu(g_l), gn_rnn),
                             rms_norm(a_l, gn_attn)], -1) @ w_out
    if not with_ctx_out:
        return o_lat, None
    a_c = context_attention(q_c, k_c, v_c, sink)
    o_ctx = jnp.concatenate([rms_norm(r_c * jax.nn.gelu(g_c), gn_rnn),
                             rms_norm(a_c, gn_attn)], -1) @ w_out
    return o_lat, o_ctx


def hier_moe(xf, wg, bg, we, be, w1, w3, w2):
    T, D = xf.shape
    g_logits = (xf @ wg + bg).astype(jnp.float32)
    g_prob = jax.nn.softmax(g_logits, -1)
    _, g_sel = lax.top_k(g_logits, 1)
    p_g = jnp.take_along_axis(g_prob, g_sel, axis=-1)
    e_logits = (xf @ we + be).astype(jnp.float32).reshape(T, N_GROUPS, EXP_PER_GROUP)
    e_in = jnp.take_along_axis(e_logits, g_sel[:, :, None], axis=1)[:, 0]
    top_v, top_i = lax.top_k(e_in, TOP_K)
    gate = (jax.nn.softmax(top_v, -1) * p_g).astype(xf.dtype)
    expert_id = (g_sel * EXP_PER_GROUP + top_i).astype(jnp.int32)
    TK = T * TOP_K
    n_blocks = (TK + N_EXPERTS * (MOE_BLK - 1) + MOE_BLK - 1) // MOE_BLK
    flat_e = expert_id.reshape(-1)
    order = jnp.argsort(flat_e)
    sorted_e = flat_e[order]
    counts = jax.ops.segment_sum(jnp.ones_like(flat_e), flat_e, num_segments=N_EXPERTS)
    starts = jnp.cumsum(counts) - counts
    pcounts = (counts + MOE_BLK - 1) // MOE_BLK * MOE_BLK
    pends = jnp.cumsum(pcounts)
    pstarts = pends - pcounts
    dest = pstarts[sorted_e] + (jnp.arange(TK, dtype=jnp.int32) - starts[sorted_e])
    tok_sorted = (order // TOP_K).astype(jnp.int32)
    slot_tok = jnp.full((n_blocks * MOE_BLK,), T, jnp.int32).at[dest].set(tok_sorted)
    x_pad = jnp.concatenate([xf, jnp.zeros((1, D), xf.dtype)], 0)
    xb = x_pad[slot_tok].reshape(n_blocks, MOE_BLK, D)
    blk_e = jnp.minimum(jnp.searchsorted(pends, jnp.arange(n_blocks) * MOE_BLK, side='right'),
                        N_EXPERTS - 1)

    def run_block(args):
        xblk, e = args
        h = jax.nn.silu(xblk @ w1[e]) * (xblk @ w3[e])
        return h @ w2[e]

    yb = lax.map(run_block, (xb, blk_e)).reshape(n_blocks * MOE_BLK, D)
    y_assign = yb[dest] * gate.reshape(-1)[order][:, None]
    return jax.ops.segment_sum(y_assign, tok_sorted, num_segments=T)


def setup_inputs(seed: int = 0) -> dict:
    key = jax.random.key(seed)
    ks = jax.random.split(key, 32)
    L = DEPTH

    def nrm(k, shape, s):
        return jax.random.normal(k, shape, jnp.float32) * s

    u = jax.random.uniform(ks[10], (L, 2, D_RNN), jnp.float32, minval=0.9, maxval=0.999)
    a0 = u ** (1.0 / RG_C)
    rg_lam = jnp.log(a0) - jnp.log1p(-a0)
    return {
        "x": nrm(ks[0], (BATCH, SEQ, D_MODEL), 1.0),
        "c": nrm(ks[1], (BATCH, D_MODEL), 1.0),
        "ctx": nrm(ks[2], (BATCH, CTX_LEN, D_MODEL), 1.0),
        "c_ctx": nrm(ks[3], (D_MODEL,), 1.0),
        "w_mod": nrm(ks[4], (L, D_MODEL, 6 * D_MODEL), 0.5 * D_MODEL ** -0.5),
        "b_mod": nrm(ks[5], (L, 6 * D_MODEL), 0.01),
        "w_in": nrm(ks[6], (L, D_MODEL, D_IN), D_MODEL ** -0.5),
        "conv_w": nrm(ks[7], (L, CONV_W, D_RNN), CONV_W ** -0.5),
        "conv_b": nrm(ks[8], (L, D_RNN), 0.01),
        "rg_wa": nrm(ks[9], (L, 2, RG_BLOCKS, RG_BD, RG_BD), RG_BD ** -0.5),
        "rg_ba": nrm(ks[11], (L, 2, D_RNN), 0.01),
        "rg_wx": nrm(ks[12], (L, 2, RG_BLOCKS, RG_BD, RG_BD), RG_BD ** -0.5),
        "rg_bx": nrm(ks[13], (L, 2, D_RNN), 0.01),
        "rg_lam": rg_lam,
        "attn_sink": nrm(ks[14], (L, N_Q_HEADS), 0.5),
        "gn_rnn": 1.0 + nrm(ks[15], (L, D_RNN), 0.02),
        "gn_attn": 1.0 + nrm(ks[16], (L, D_ATTN), 0.02),
        "w_out": nrm(ks[17], (L, D_MIX, D_MODEL), BETA * D_MIX ** -0.5),
        "ln1_g": 1.0 + nrm(ks[18], (L, D_MODEL), 0.02),
        "ln1_b": nrm(ks[19], (L, D_MODEL), 0.01),
        "router_g": nrm(ks[20], (L, D_MODEL, N_GROUPS), D_MODEL ** -0.5),
        "router_gb": nrm(ks[21], (L, N_GROUPS), 0.01),
        "router_e": nrm(ks[22], (L, D_MODEL, N_EXPERTS), D_MODEL ** -0.5),
        "router_eb": nrm(ks[23], (L, N_EXPERTS), 0.01),
        "w1": nrm(ks[24], (L, N_EXPERTS, D_MODEL, D_EXPERT), D_MODEL ** -0.5),
        "w3": nrm(ks[25], (L, N_EXPERTS, D_MODEL, D_EXPERT), D_MODEL ** -0.5),
        "w2": nrm(ks[26], (L, N_EXPERTS, D_EXPERT, D_MODEL), BETA * D_EXPERT ** -0.5),
        "ln2_g": 1.0 + nrm(ks[27], (L, D_MODEL), 0.02),
        "ln2_b": nrm(ks[28], (L, D_MODEL), 0.01),
    }


def reference(x, c, ctx, c_ctx, w_mod, b_mod, w_in, conv_w, conv_b, rg_wa, rg_ba, rg_wx, rg_bx,
              rg_lam, attn_sink, gn_rnn, gn_attn, w_out, ln1_g, ln1_b, router_g, router_gb,
              router_e, router_eb, w1, w3, w2, ln2_g, ln2_b):
    B, S, D = x.shape
    rope = axial_rope_tables(S)
    silu_c = jax.nn.silu(c)
    silu_cc = jax.nn.silu(c_ctx)
    xc = ctx
    for l in range(DEPTH):
        last = l == DEPTH - 1
        m_lat = silu_c @ w_mod[l] + b_mod[l]
        m_ctx = silu_cc @ w_mod[l] + b_mod[l]
        sh1, sc1, g1, sh2, sc2, g2 = jnp.split(m_lat[:, None, :], 6, axis=-1)
        sh1c, sc1c, g1c, sh2c, sc2c, g2c = jnp.split(m_ctx, 6)
        h_lat = modulate(x, sh1, sc1)
        h_ctx = modulate(xc, sh1c, sc1c)
        y_lat, y_ctx = token_mixer(h_lat, h_ctx, rope, w_in[l], conv_w[l], conv_b[l], rg_wa[l],
                                   rg_ba[l], rg_wx[l], rg_bx[l], rg_lam[l], attn_sink[l],
                                   gn_rnn[l], gn_attn[l], w_out[l], not last)
        x = layer_norm(ALPHA * x + g1 * y_lat, ln1_g[l], ln1_b[l])
        h_lat = modulate(x, sh2, sc2).reshape(B * S, D)
        if last:
            y = hier_moe(h_lat, router_g[l], router_gb[l], router_e[l], router_eb[l],
                         w1[l], w3[l], w2[l])
            x = layer_norm(ALPHA * x + g2 * y.reshape(B, S, D), ln2_g[l], ln2_b[l])
        else:
            xc = layer_norm(ALPHA * xc + g1c * y_ctx, ln1_g[l], ln1_b[l])
            h_ctx = modulate(xc, sh2c, sc2c).reshape(-1, D)
            y = hier_moe(jnp.concatenate([h_lat, h_ctx], 0), router_g[l], router_gb[l],
                         router_e[l], router_eb[l], w1[l], w3[l], w2[l])
            x = layer_norm(ALPHA * x + g2 * y[:B * S].reshape(B, S, D), ln2_g[l], ln2_b[l])
            xc = layer_norm(ALPHA * xc + g2c * y[B * S:].reshape(xc.shape), ln2_g[l], ln2_b[l])
    return x
```

```python
import functools

import jax
import jax.numpy as jnp
from jax import lax
from jax.experimental import pallas as pl
from jax.experimental.pallas import tpu as pltpu

D_MODEL = 1024
D_RNN = 512
D_ATTN = 512
HEAD_DIM = 64
N_Q_HEADS = 8
N_KV_HEADS = 2
GQ = N_Q_HEADS // N_KV_HEADS
D_KV = N_KV_HEADS * HEAD_DIM
WINDOW = 128
BLK = 128
ATTN_SCALE = HEAD_DIM ** -0.5
AXIS_DIM = HEAD_DIM // 2
ROPE_BASE = 10000.0
GRID_W = 64
RG_BLOCKS = 8
RG_BD = D_RNN // RG_BLOCKS
RG_C = 8.0
CONV_W = 4
CONV_LEFT = 2
N_GROUPS = 4
EXP_PER_GROUP = 8
N_EXPERTS = N_GROUPS * EXP_PER_GROUP
TOP_K = 2
D_EXPERT = 512
D_IN = 2 * D_RNN + D_ATTN + 2 * D_KV
LN_EPS = 1e-6
NEG_INF = -1e30

LANES = 128
SUBLANES = 8
ROW_TILE = 256
SCAN_TILE = 128
MOE_TILE = 256
LOGIT_W = LANES
VMEM_LIMIT = 56 * 1024 * 1024

_F32 = jnp.float32
_BF16 = jnp.bfloat16


def _cparams(*sem):
    return pltpu.CompilerParams(dimension_semantics=sem, vmem_limit_bytes=VMEM_LIMIT)


def _sigmoid(x):
    return 1.0 / (1.0 + jnp.exp(-x))


def _dot(a, b):
    return jnp.dot(a, b, preferred_element_type=_F32)


def _mod_kernel(c_ref, w_ref, b_ref, o_ref):
    c = c_ref[...]
    a = c * _sigmoid(c)
    a_hi = a.astype(_BF16)
    a_lo = (a - a_hi.astype(_F32)).astype(_BF16)
    w = w_ref[0]
    w_hi = w.astype(_BF16)
    w_lo = (w - w_hi.astype(_F32)).astype(_BF16)
    o_ref[0] = _dot(a_hi, w_hi) + (_dot(a_lo, w_hi) + _dot(a_hi, w_lo)) + b_ref[0]


def _modulation(c_all, w_mod, b_mod):
    depth, d, n = w_mod.shape
    rows = c_all.shape[0]
    tn = 1536
    return pl.pallas_call(
        _mod_kernel,
        grid=(depth, n // tn),
        in_specs=[
            pl.BlockSpec((rows, d), lambda l, j: (0, 0)),
            pl.BlockSpec((1, d, tn), lambda l, j: (l, 0, j)),
            pl.BlockSpec((1, 1, tn), lambda l, j: (l, 0, j)),
        ],
        out_specs=pl.BlockSpec((1, rows, tn), lambda l, j: (l, 0, j)),
        out_shape=jax.ShapeDtypeStruct((depth, rows, n), _F32),
        compiler_params=_cparams("arbitrary", "arbitrary"),
        name="modulation",
    )(c_all, w_mod, b_mod.reshape(depth, 1, n))


def _normalize(x):
    mu = jnp.mean(x, axis=-1, keepdims=True)
    xc = x - mu
    var = jnp.mean(xc * xc, axis=-1, keepdims=True)
    return xc * lax.rsqrt(var + LN_EPS)


def _rope128(x, cos, sin_a, sin_b):
    return (x * cos + pltpu.roll(x, LANES - AXIS_DIM // 2, 1) * sin_a
            + pltpu.roll(x, AXIS_DIM // 2, 1) * sin_b)


def _mod_index(b, j, n_lat_tiles, n_batch):
    return jnp.where(j >= n_lat_tiles, n_batch, b)


def _inproj_kernel(x_ref, mod_ref, w_ref, cos_ref, sa_ref, sb_ref,
                   u_ref, g_ref, q_ref, k_ref, v_ref):
    h = _normalize(x_ref[0]) * (1.0 + mod_ref[0, 1:2, :]) + mod_ref[0, 0:1, :]
    y = _dot(h.astype(_BF16), w_ref[...])
    u_ref[0] = y[:, :D_RNN]
    g_ref[0] = y[:, D_RNN:2 * D_RNN]
    cos, sa, sb = cos_ref[...], sa_ref[...], sb_ref[...]
    q0 = 2 * D_RNN
    for c in range(D_ATTN // LANES):
        qc = _rope128(y[:, q0 + c * LANES:q0 + (c + 1) * LANES], cos, sa, sb)
        q_ref[0, :, c * LANES:(c + 1) * LANES] = (qc * ATTN_SCALE).astype(_BF16)
    k0 = q0 + D_ATTN
    k_ref[0] = _rope128(y[:, k0:k0 + D_KV], cos, sa, sb).astype(_BF16)
    v_ref[0] = y[:, k0 + D_KV:k0 + 2 * D_KV].astype(_BF16)


def _inproj(x, mods, w_in, rope, n_lat_tiles):
    nb, r, d = x.shape
    tm = ROW_TILE
    nj = r // tm
    cos, sa, sb = rope
    rope_spec = pl.BlockSpec((tm, LANES), lambda b, j: (j, 0))
    row = lambda w: pl.BlockSpec((1, tm, w), lambda b, j: (b, j, 0))
    return pl.pallas_call(
        _inproj_kernel,
        grid=(nb, nj),
        in_specs=[
            row(d),
            pl.BlockSpec((1, 6, d), lambda b, j: (_mod_index(b, j, n_lat_tiles, nb), 0, 0)),
            pl.BlockSpec((d, D_IN), lambda b, j: (0, 0)),
            rope_spec, rope_spec, rope_spec,
        ],
        out_specs=[row(D_RNN), row(D_RNN), row(D_ATTN), row(D_KV), row(D_KV)],
        out_shape=[
            jax.ShapeDtypeStruct((nb, r, D_RNN), _F32),
            jax.ShapeDtypeStruct((nb, r, D_RNN), _F32),
            jax.ShapeDtypeStruct((nb, r, D_ATTN), _BF16),
            jax.ShapeDtypeStruct((nb, r, D_KV), _BF16),
            jax.ShapeDtypeStruct((nb, r, D_KV), _BF16),
        ],
        compiler_params=_cparams("arbitrary", "arbitrary"),
        name="inproj",
    )(x, mods, w_in, cos, sa, sb)


def _scan_block(step, reverse, n_lat, n_ctx):
    in_ctx = step < n_ctx
    i_ctx = (n_ctx - 1 - step) if reverse else step
    s_lat = step - n_ctx
    i_lat = (n_lat - 1 - s_lat) if reverse else s_lat
    blk = jnp.where(in_ctx, n_lat + i_ctx, i_lat)
    first = jnp.where(in_ctx, i_ctx == 0, i_lat == 0)
    last = jnp.where(in_ctx, i_ctx == n_ctx - 1, i_lat == n_lat - 1)
    return blk, first, last


def _rglru_kernel(reverse, n_lat, n_ctx, *refs):
    if reverse:
        (up_ref, uc_ref, un_ref, cw_ref, cb_ref, w_ref, bias_ref, lam_ref,
         hf_ref, g_ref, gn_ref, o_ref, ubuf, a_s, b_s, h_s, carry) = refs
    else:
        (up_ref, uc_ref, un_ref, cw_ref, cb_ref, w_ref, bias_ref, lam_ref,
         o_ref, ubuf, a_s, b_s, h_s, carry) = refs
    nb, ts, ch = uc_ref.shape
    step = pl.program_id(0)
    _, first, last = _scan_block(step, reverse, n_lat, n_ctx)

    @pl.when(step == 0)
    def _():
        carry[...] = jnp.zeros_like(carry)

    ubuf[:, SUBLANES:SUBLANES + ts, :] = uc_ref[...]
    ubuf[:, 0:SUBLANES, :] = jnp.where(first, 0.0, up_ref[...])
    ubuf[:, SUBLANES + ts:, :] = jnp.where(last, 0.0, un_ref[...])
    x = cb_ref[...][None]
    for j in range(CONV_W):
        off = SUBLANES + j - CONV_LEFT
        x = x + ubuf[:, off:off + ts, :] * cw_ref[j:j + 1, :][None]
    x = x.reshape(nb * ts, ch)

    gates = _sigmoid(_dot(x.astype(_BF16), w_ref[...]) + bias_ref[...])
    r = gates[:, :ch]
    gi = gates[:, ch:]
    neg_lam = -lam_ref[...]
    softplus = jnp.maximum(neg_lam, 0.0) + jnp.log1p(jnp.exp(-jnp.abs(neg_lam)))
    log_a = (-RG_C) * r * softplus
    a = jnp.exp(log_a)
    b = jnp.sqrt(jnp.tanh(-log_a) * (1.0 + a * a)) * (gi * x)
    n_chunk = ch // LANES
    for c in range(n_chunk):
        a_s[c] = a[:, c * LANES:(c + 1) * LANES]
        b_s[c] = b[:, c * LANES:(c + 1) * LANES]

    def body(i, hs):
        t = (ts - 1 - i) if reverse else i
        rows = pl.ds(t, nb, stride=ts)
        new = []
        for c in range(n_chunk):
            hc = a_s[c, rows, :] * hs[c] + b_s[c, rows, :]
            h_s[c, rows, :] = hc
            new.append(hc)
        return tuple(new)

    h0 = tuple(carry[c] for c in range(n_chunk))
    h_end = lax.fori_loop(0, ts, body, h0, unroll=8)
    for c in range(n_chunk):
        carry[c] = h_end[c]

    h = jnp.concatenate([h_s[c] for c in range(n_chunk)], axis=-1).reshape(nb, ts, ch)
    if reverse:
        g = g_ref[...]
        gelu = 0.5 * g * (1.0 + jnp.tanh(0.7978845608028654 * (g + 0.044715 * (g * g * g))))
        y = (hf_ref[...] + h) * gelu
        ms = jnp.mean(y * y, axis=-1, keepdims=True)
        o_ref[...] = (y * lax.rsqrt(ms + LN_EPS) * gn_ref[...][None]).astype(o_ref.dtype)
    else:
        o_ref[...] = h


def _rglru(u, conv_w, conv_b, w_gate, b_gate, lam, n_lat, n_ctx, reverse, hf=None, g=None, gn=None):
    nb, r, ch = u.shape
    ts = SCAN_TILE
    halo = ts // SUBLANES
    n_halo = r // SUBLANES

    def cur(s):
        return (0, _scan_block(s, reverse, n_lat, n_ctx)[0], 0)

    def prev(s):
        return (0, jnp.maximum(_scan_block(s, reverse, n_lat, n_ctx)[0] * halo - 1, 0), 0)

    def nxt(s):
        return (0, jnp.minimum((_scan_block(s, reverse, n_lat, n_ctx)[0] + 1) * halo, n_halo - 1), 0)

    full = lambda shape: pl.BlockSpec(shape, lambda s: (0,) * len(shape))
    blk = pl.BlockSpec((nb, ts, ch), cur)
    in_specs = [
        pl.BlockSpec((nb, SUBLANES, ch), prev), blk, pl.BlockSpec((nb, SUBLANES, ch), nxt),
        full((CONV_W, ch)), full((1, ch)), full((ch, 2 * ch)), full((1, 2 * ch)), full((1, ch)),
    ]
    args = [u, u, u, conv_w, conv_b, w_gate, b_gate, lam]
    if reverse:
        in_specs += [blk, blk, full((1, ch))]
        args += [hf, g, gn]
    return pl.pallas_call(
        functools.partial(_rglru_kernel, reverse, n_lat, n_ctx),
        grid=(n_lat + n_ctx,),
        in_specs=in_specs,
        out_specs=blk,
        out_shape=jax.ShapeDtypeStruct((nb, r, ch), _BF16 if reverse else _F32),
        scratch_shapes=[
            pltpu.VMEM((nb, ts + 2 * SUBLANES, ch), _F32),
            pltpu.VMEM((ch // LANES, nb * ts, LANES), _F32),
            pltpu.VMEM((ch // LANES, nb * ts, LANES), _F32),
            pltpu.VMEM((ch // LANES, nb * ts, LANES), _F32),
            pltpu.VMEM((ch // LANES, nb, LANES), _F32),
        ],
        compiler_params=_cparams("arbitrary"),
        name="rglru_bwd" if reverse else "rglru_fwd",
    )(*args)


def _attn_kernel(n_lat, sink_ref, q_ref, kp_ref, kc_ref, kn_ref, kx_ref,
                 vp_ref, vc_ref, vn_ref, vx_ref, gn_ref, o_ref):
    n = pl.program_id(1)
    n_ctx_keys = kx_ref.shape[1]
    n_keys = 3 * BLK + n_ctx_keys
    is_lat = n < n_lat
    lo = jnp.where(is_lat, jnp.where(n >= 1, 0, BLK), 3 * BLK)
    hi = jnp.where(is_lat, jnp.where(n + 1 < n_lat, 3 * BLK, 2 * BLK), 0)
    col = lax.broadcasted_iota(jnp.int32, (BLK, n_keys), 1)
    row = lax.broadcasted_iota(jnp.int32, (BLK, n_keys), 0)
    in_win = jnp.abs(col - BLK - row) <= WINDOW
    mask = ((col >= lo) & (col < hi) & in_win) | (col >= 3 * BLK)

    q = q_ref[0]
    k_all = jnp.concatenate([kp_ref[0], kc_ref[0], kn_ref[0], kx_ref[0]], axis=0)
    v_all = jnp.concatenate([vp_ref[0], vc_ref[0], vn_ref[0], vx_ref[0]], axis=0)
    outs = []
    for hq in range(N_Q_HEADS):
        hk = hq // GQ
        qh = q[:, hq * HEAD_DIM:(hq + 1) * HEAD_DIM]
        kh = k_all[:, hk * HEAD_DIM:(hk + 1) * HEAD_DIM]
        vh = v_all[:, hk * HEAD_DIM:(hk + 1) * HEAD_DIM]
        s = lax.dot_general(qh, kh, (((1,), (1,)), ((), ())), preferred_element_type=_F32)
        s = jnp.where(mask, s, NEG_INF)
        sink = sink_ref[hq]
        m = jnp.maximum(jnp.max(s, axis=-1, keepdims=True), sink)
        p = jnp.exp(s - m)
        denom = jnp.sum(p, axis=-1, keepdims=True) + jnp.exp(sink - m)
        outs.append(_dot(p.astype(_BF16), vh) / denom)
    o = jnp.concatenate(outs, axis=-1)
    ms = jnp.mean(o * o, axis=-1, keepdims=True)
    o_ref[0] = (o * lax.rsqrt(ms + LN_EPS) * gn_ref[...]).astype(o_ref.dtype)


def _attention(q, k, v, sink, gn, n_lat, ctx_len):
    nb, r, _ = q.shape
    nq = r // BLK
    ctx_blk = (n_lat * BLK) // ctx_len
    band = lambda f: pl.BlockSpec((1, BLK, D_KV), lambda b, n: (b, f(n), 0))
    prev = band(lambda n: jnp.clip(n - 1, 0, n_lat - 1))
    cur = band(lambda n: jnp.minimum(n, n_lat - 1))
    nxt = band(lambda n: jnp.clip(n + 1, 0, n_lat - 1))
    ctx = pl.BlockSpec((1, ctx_len, D_KV), lambda b, n: (b, ctx_blk, 0))
    return pl.pallas_call(
        functools.partial(_attn_kernel, n_lat),
        grid=(nb, nq),
        in_specs=[
            pl.BlockSpec(memory_space=pltpu.SMEM),
            pl.BlockSpec((1, BLK, D_ATTN), lambda b, n: (b, n, 0)),
            prev, cur, nxt, ctx, prev, cur, nxt, ctx,
            pl.BlockSpec((1, D_ATTN), lambda b, n: (0, 0)),
        ],
        out_specs=pl.BlockSpec((1, BLK, D_ATTN), lambda b, n: (b, n, 0)),
        out_shape=jax.ShapeDtypeStruct((nb, r, D_ATTN), _BF16),
        compiler_params=_cparams("arbitrary", "arbitrary"),
        name="attention",
    )(sink, q, k, k, k, k, v, v, v, v, gn)


def _outproj_kernel(alpha, rn_ref, an_ref, x_ref, mod_ref, wr_ref, wa_ref, lg_ref, lb_ref,
                    wrt_ref, brt_ref, x1_ref, h2_ref, lo_ref):
    y = _dot(rn_ref[0], wr_ref[...]) + _dot(an_ref[0], wa_ref[...])
    z = alpha * x_ref[0] + mod_ref[0, 2:3, :] * y
    x1 = _normalize(z) * lg_ref[...] + lb_ref[...]
    x1_ref[0] = x1
    h2 = _normalize(x1) * (1.0 + mod_ref[0, 4:5, :]) + mod_ref[0, 3:4, :]
    h2_ref[0] = h2
    lo_ref[0] = _dot(h2.astype(_BF16), wrt_ref[...]) + brt_ref[...]


def _outproj(rn, an, x, mods, w_out, ln_g, ln_b, w_rt, b_rt, n_lat_tiles, alpha):
    nb, r, d = x.shape
    tm = ROW_TILE
    row = lambda w: pl.BlockSpec((1, tm, w), lambda b, j: (b, j, 0))
    full = lambda shape: pl.BlockSpec(shape, lambda b, j: (0,) * len(shape))
    return pl.pallas_call(
        functools.partial(_outproj_kernel, alpha),
        grid=(nb, r // tm),
        in_specs=[
            row(D_RNN), row(D_ATTN), row(d),
            pl.BlockSpec((1, 6, d), lambda b, j: (_mod_index(b, j, n_lat_tiles, nb), 0, 0)),
            pl.BlockSpec((D_RNN, d), lambda b, j: (0, 0)),
            pl.BlockSpec((D_ATTN, d), lambda b, j: (1, 0)),
            full((1, d)), full((1, d)), full((d, LOGIT_W)), full((1, LOGIT_W)),
        ],
        out_specs=[row(d), row(d), row(LOGIT_W)],
        out_shape=[
            jax.ShapeDtypeStruct((nb, r, d), _F32),
            jax.ShapeDtypeStruct((nb, r, d), _F32),
            jax.ShapeDtypeStruct((nb, r, LOGIT_W), _F32),
        ],
        compiler_params=_cparams("arbitrary", "arbitrary"),
        name="outproj",
    )(rn, an, x, mods, w_out, w_out, ln_g, ln_b, w_rt, b_rt)


def _row_gather(idx_ref, n_rows, src_hbm, dst, sem):
    def body(i, carry):
        t = idx_ref[0, 0, i]
        pltpu.make_async_copy(src_hbm.at[pl.ds(t, 1), :], dst.at[pl.ds(i, 1), :], sem).start()
        return carry

    lax.fori_loop(0, n_rows, body, 0)


def _moe_kernel(blk_e_ref, tok_ref, tok_next_ref, h_hbm, w1_ref, w3_ref, w2_ref, o_ref, xbuf, sem):
    del blk_e_ref
    j = pl.program_id(0)
    n = pl.num_programs(0)
    slot = j % 2
    bm = xbuf.shape[1]

    @pl.when(j == 0)
    def _():
        _row_gather(tok_ref, bm, h_hbm, xbuf.at[0], sem.at[0])

    @pl.when(j + 1 < n)
    def _():
        _row_gather(tok_next_ref, bm, h_hbm, xbuf.at[1 - slot], sem.at[1 - slot])

    pltpu.make_async_copy(h_hbm.at[pl.ds(0, bm), :], xbuf.at[slot], sem.at[slot]).wait()
    x = xbuf[slot].astype(_BF16)
    a = _dot(x, w1_ref[0])
    h = (a * _sigmoid(a)) * _dot(x, w3_ref[0])
    o_ref[...] = _dot(h.astype(_BF16), w2_ref[0])


def _moe_ffn(h2, slot_tok, blk_e, w1, w3, w2):
    nt, d = h2.shape
    bm = MOE_TILE
    n_blk = blk_e.shape[0]
    tok3 = slot_tok.reshape(n_blk, 1, bm)
    smem_blk = lambda f: pl.BlockSpec((1, 1, bm), lambda j, be: (f(j), 0, 0), memory_space=pltpu.SMEM)
    grid_spec = pltpu.PrefetchScalarGridSpec(
        num_scalar_prefetch=1,
        grid=(n_blk,),
        in_specs=[
            smem_blk(lambda j: j),
            smem_blk(lambda j: jnp.minimum(j + 1, n_blk - 1)),
            pl.BlockSpec(memory_space=pl.ANY),
            pl.BlockSpec((1, d, D_EXPERT), lambda j, be: (be[j], 0, 0)),
            pl.BlockSpec((1, d, D_EXPERT), lambda j, be: (be[j], 0, 0)),
            pl.BlockSpec((1, D_EXPERT, d), lambda j, be: (be[j], 0, 0)),
        ],
        out_specs=pl.BlockSpec((bm, d), lambda j, be: (j, 0)),
        scratch_shapes=[pltpu.VMEM((2, bm, d), _F32), pltpu.SemaphoreType.DMA((2,))],
    )
    return pl.pallas_call(
        _moe_kernel,
        grid_spec=grid_spec,
        out_shape=jax.ShapeDtypeStruct((n_blk * bm, d), _F32),
        compiler_params=_cparams("arbitrary"),
        name="moe_ffn",
    )(blk_e, tok3, tok3, h2, w1, w3, w2)


def _combine_kernel(alpha, dst_ref, dst_next_ref, y_hbm, gate_ref, x_ref, mod_ref, lg_ref, lb_ref,
                    o_ref, ybuf, sem):
    s = pl.program_id(0)
    n = pl.num_programs(0)
    slot = s % 2
    tm = x_ref.shape[0]

    @pl.when(s == 0)
    def _():
        _row_gather(dst_ref, TOP_K * tm, y_hbm, ybuf.at[0], sem.at[0])

    @pl.when(s + 1 < n)
    def _():
        _row_gather(dst_next_ref, TOP_K * tm, y_hbm, ybuf.at[1 - slot], sem.at[1 - slot])

    pltpu.make_async_copy(y_hbm.at[pl.ds(0, TOP_K * tm), :], ybuf.at[slot], sem.at[slot]).wait()
    gate = gate_ref[...]
    y = ybuf[slot, 0:tm, :] * gate[:, 0:1] + ybuf[slot, tm:2 * tm, :] * gate[:, 1:2]
    z = alpha * x_ref[...] + mod_ref[0, 5:6, :] * y
    o_ref[...] = _normalize(z) * lg_ref[...] + lb_ref[...]


def _combine(yb, dest, gate, x1, mods, ln_g, ln_b, n_batch, tiles_per_batch, tiles_used, alpha):
    nt, d = x1.shape
    tm = ROW_TILE
    n_steps = n_batch * tiles_used
    dst3 = dest.reshape(nt // tm, tm, TOP_K).transpose(0, 2, 1).reshape(nt // tm, 1, TOP_K * tm)

    def tile(s):
        return (s // tiles_used) * tiles_per_batch + s % tiles_used

    def mod_idx(s):
        return (jnp.where(s % tiles_used >= tiles_per_batch - 1, n_batch, s // tiles_used), 0, 0)

    smem_blk = lambda f: pl.BlockSpec((1, 1, TOP_K * tm), lambda s: (tile(f(s)), 0, 0),
                                      memory_space=pltpu.SMEM)
    full = lambda shape: pl.BlockSpec(shape, lambda s: (0,) * len(shape))
    return pl.pallas_call(
        functools.partial(_combine_kernel, alpha),
        grid=(n_steps,),
        in_specs=[
            smem_blk(lambda s: s),
            smem_blk(lambda s: jnp.minimum(s + 1, n_steps - 1)),
            pl.BlockSpec(memory_space=pl.ANY),
            pl.BlockSpec((tm, TOP_K), lambda s: (tile(s), 0)),
            pl.BlockSpec((tm, d), lambda s: (tile(s), 0)),
            pl.BlockSpec((1, 6, d), mod_idx),
            full((1, d)), full((1, d)),
        ],
        out_specs=pl.BlockSpec((tm, d), lambda s: (s, 0)),
        out_shape=jax.ShapeDtypeStruct((n_steps * tm, d), _F32),
        scratch_shapes=[pltpu.VMEM((2, TOP_K * tm, d), _F32), pltpu.SemaphoreType.DMA((2,))],
        compiler_params=_cparams("arbitrary"),
        name="combine_ln2",
    )(dst3, dst3, yb, gate, x1, mods, ln_g, ln_b)


def _route(logits, n_blk):
    nt = logits.shape[0]
    bm = MOE_TILE
    g_logits = logits[:, :N_GROUPS]
    e_logits = logits[:, N_GROUPS:N_GROUPS + N_EXPERTS].reshape(nt, N_GROUPS, EXP_PER_GROUP)
    g_prob = jax.nn.softmax(g_logits, -1)
    _, g_sel = lax.top_k(g_logits, 1)
    p_g = jnp.take_along_axis(g_prob, g_sel, axis=-1)
    e_in = jnp.take_along_axis(e_logits, g_sel[:, :, None], axis=1)[:, 0]
    top_v, top_i = lax.top_k(e_in, TOP_K)
    gate = jax.nn.softmax(top_v, -1) * p_g
    flat_e = (g_sel * EXP_PER_GROUP + top_i).astype(jnp.int32).reshape(-1)
    n_assign = nt * TOP_K
    onehot = (flat_e[:, None] == jnp.arange(N_EXPERTS, dtype=jnp.int32)[None, :]).astype(jnp.int32)
    csum = jnp.cumsum(onehot, axis=0)
    rank = jnp.take_along_axis(csum, flat_e[:, None], axis=1)[:, 0] - 1
    counts = csum[-1]
    pcounts = (counts + bm - 1) // bm * bm
    pends = jnp.cumsum(pcounts)
    pstarts = pends - pcounts
    dest = (pstarts[flat_e] + rank).astype(jnp.int32)
    n_fill = n_blk * bm - n_assign
    pad_ends = jnp.cumsum(pcounts - counts)
    fill_key = jnp.searchsorted(pad_ends, jnp.arange(n_fill, dtype=jnp.int32), side="right").astype(jnp.int32)
    keys = jnp.concatenate([flat_e, fill_key])
    vals = jnp.concatenate([jnp.arange(n_assign, dtype=jnp.int32) // TOP_K, jnp.zeros((n_fill,), jnp.int32)])
    _, slot_tok = lax.sort_key_val(keys, vals)
    blk_e = jnp.minimum(jnp.searchsorted(pends, jnp.arange(n_blk, dtype=jnp.int32) * bm, side="right"),
                        N_EXPERTS - 1).astype(jnp.int32)
    return dest.reshape(nt, TOP_K), gate, slot_tok, blk_e


def _rope_tables(seq, ident_rows):
    rows = seq // GRID_W
    row = jnp.repeat(jnp.arange(rows, dtype=_F32), GRID_W)
    col = jnp.tile(jnp.arange(GRID_W, dtype=_F32), rows)
    inv = ROPE_BASE ** (-jnp.arange(0, AXIS_DIM, 2, dtype=_F32) / AXIS_DIM)
    ang_r = row[:, None] * inv
    ang_c = col[:, None] * inv
    cr, sr, cc, sc = jnp.cos(ang_r), jnp.sin(ang_r), jnp.cos(ang_c), jnp.sin(ang_c)
    z = jnp.zeros_like(sr)
    reps = LANES // HEAD_DIM
    cos = jnp.tile(jnp.concatenate([cr, cr, cc, cc], -1), (1, reps))
    sin_a = jnp.tile(jnp.concatenate([-sr, z, -sc, z], -1), (1, reps))
    sin_b = jnp.tile(jnp.concatenate([z, sr, z, sc], -1), (1, reps))
    one = jnp.ones((ident_rows, LANES), _F32)
    zero = jnp.zeros((ident_rows, LANES), _F32)
    return (jnp.concatenate([cos, one], 0), jnp.concatenate([sin_a, zero], 0),
            jnp.concatenate([sin_b, zero], 0))


def _block_diag(w):
    eye = jnp.eye(RG_BLOCKS, dtype=w.dtype)
    return jnp.einsum("hij,hg->higj", w, eye).reshape(D_RNN, D_RNN)


def kernel(x, c, ctx, c_ctx, w_mod, b_mod, w_in, conv_w, conv_b, rg_wa, rg_ba, rg_wx, rg_bx, rg_lam,
           attn_sink, gn_rnn, gn_attn, w_out, ln1_g, ln1_b, router_g, router_gb, router_e, router_eb,
           w1, w3, w2, ln2_g, ln2_b):
    nb, seq, d = x.shape
    ctx_len = ctx.shape[1]
    depth = w_mod.shape[0]
    r = seq + ctx_len
    nt = nb * r
    alpha = (2 * depth) ** 0.25
    assert d == D_MODEL and seq % ROW_TILE == 0 and ctx_len % ROW_TILE == 0
    assert seq % GRID_W == 0 and seq % ctx_len == 0 and ctx_len % BLK == 0
    n_lat_tiles = seq // ROW_TILE
    tiles_per_batch = r // ROW_TILE
    assert tiles_per_batch == n_lat_tiles + 1, "one context tile per batch"
    n_blk = (nt * TOP_K + N_EXPERTS * (MOE_TILE - 1) + MOE_TILE - 1) // MOE_TILE

    pad_rows = (-(nb + 1)) % SUBLANES
    c_all = jnp.concatenate([c, c_ctx[None, :], jnp.zeros((pad_rows, d), _F32)], 0)
    mods_all = _modulation(c_all, w_mod, b_mod).reshape(depth, nb + 1 + pad_rows, 6, d)

    rope = _rope_tables(seq, ROW_TILE)
    xs = jnp.concatenate([x, ctx], axis=1)

    out = None
    for l in range(depth):
        last = l == depth - 1
        mods = mods_all[l]
        u, g, q, k, v = _inproj(xs, mods, w_in[l].astype(_BF16), rope, n_lat_tiles)

        n_lat_s, n_ctx_s = seq // SCAN_TILE, ctx_len // SCAN_TILE
        hf = None
        for dr in range(2):
            w_gate = jnp.concatenate([_block_diag(rg_wa[l, dr]), _block_diag(rg_wx[l, dr])], 1).astype(_BF16)
            b_gate = jnp.concatenate([rg_ba[l, dr], rg_bx[l, dr]])[None, :]
            args = (u, conv_w[l], conv_b[l][None, :], w_gate, b_gate, rg_lam[l, dr][None, :], n_lat_s, n_ctx_s)
            if dr == 0:
                hf = _rglru(*args, reverse=False)
            else:
                rn = _rglru(*args, reverse=True, hf=hf, g=g, gn=gn_rnn[l][None, :])

        an = _attention(q, k, v, attn_sink[l], gn_attn[l][None, :], seq // BLK, ctx_len)

        w_rt = jnp.concatenate([router_g[l], router_e[l],
                                jnp.zeros((d, LOGIT_W - N_GROUPS - N_EXPERTS), _F32)], 1).astype(_BF16)
        b_rt = jnp.concatenate([router_gb[l], router_eb[l],
                                jnp.zeros((LOGIT_W - N_GROUPS - N_EXPERTS,), _F32)])[None, :]
        x1, h2, logits = _outproj(rn, an, xs, mods, w_out[l].astype(_BF16), ln1_g[l][None, :],
                                  ln1_b[l][None, :], w_rt, b_rt, n_lat_tiles, alpha)

        dest, gate, slot_tok, blk_e = _route(logits.reshape(nt, LOGIT_W), n_blk)
        yb = _moe_ffn(h2.reshape(nt, d), slot_tok, blk_e, w1[l].astype(_BF16), w3[l].astype(_BF16),
                      w2[l].astype(_BF16))
        tiles_used = n_lat_tiles if last else tiles_per_batch
        x2 = _combine(yb, dest, gate, x1.reshape(nt, d), mods, ln2_g[l][None, :], ln2_b[l][None, :],
                      nb, tiles_per_batch, tiles_used, alpha)
        if last:
            out = x2.reshape(nb, seq, d)
        else:
            xs = x2.reshape(nb, r, d)
    return out
```

```python
import functools

import jax
import jax.numpy as jnp
from jax import lax
from jax.experimental import pallas as pl
from jax.experimental.pallas import tpu as pltpu

D_MODEL = 1024
D_RNN = 512
D_ATTN = 512
HEAD_DIM = 64
N_Q_HEADS = 8
N_KV_HEADS = 2
GQ = N_Q_HEADS // N_KV_HEADS
D_KV = N_KV_HEADS * HEAD_DIM
WINDOW = 128
BLK = 128
ATTN_SCALE = HEAD_DIM ** -0.5
AXIS_DIM = HEAD_DIM // 2
ROPE_BASE = 10000.0
GRID_W = 64
RG_BLOCKS = 8
RG_BD = D_RNN // RG_BLOCKS
RG_C = 8.0
CONV_W = 4
CONV_LEFT = 2
N_GROUPS = 4
EXP_PER_GROUP = 8
N_EXPERTS = N_GROUPS * EXP_PER_GROUP
TOP_K = 2
D_EXPERT = 512
D_IN = 2 * D_RNN + D_ATTN + 2 * D_KV
LN_EPS = 1e-6
NEG_INF = -1e30

LANES = 128
SUBLANES = 8
ROW_TILE = 256
SCAN_TILE = 128
MOE_TILE = 256
LOGIT_W = LANES
VMEM_LIMIT = 56 * 1024 * 1024

_F32 = jnp.float32
_BF16 = jnp.bfloat16


def _cparams(*sem):
    return pltpu.CompilerParams(dimension_semantics=sem, vmem_limit_bytes=VMEM_LIMIT)


def _sigmoid(x):
    return 1.0 / (1.0 + jnp.exp(-x))


def _dot(a, b):
    return jnp.dot(a, b, preferred_element_type=_F32)


def _mod_kernel(c_ref, w_ref, b_ref, o_ref):
    c = c_ref[...]
    a = c * _sigmoid(c)
    a_hi = a.astype(_BF16)
    a_lo = (a - a_hi.astype(_F32)).astype(_BF16)
    w = w_ref[0]
    w_hi = w.astype(_BF16)
    w_lo = (w - w_hi.astype(_F32)).astype(_BF16)
    o_ref[0] = _dot(a_hi, w_hi) + (_dot(a_lo, w_hi) + _dot(a_hi, w_lo)) + b_ref[0]


def _modulation(c_all, w_mod, b_mod):
    depth, d, n = w_mod.shape
    rows = c_all.shape[0]
    tn = 1536
    return pl.pallas_call(
        _mod_kernel,
        grid=(depth, n // tn),
        in_specs=[
            pl.BlockSpec((rows, d), lambda l, j: (0, 0)),
            pl.BlockSpec((1, d, tn), lambda l, j: (l, 0, j)),
            pl.BlockSpec((1, 1, tn), lambda l, j: (l, 0, j)),
        ],
        out_specs=pl.BlockSpec((1, rows, tn), lambda l, j: (l, 0, j)),
        out_shape=jax.ShapeDtypeStruct((depth, rows, n), _F32),
        compiler_params=_cparams("arbitrary", "arbitrary"),
        name="modulation",
    )(c_all, w_mod, b_mod.reshape(depth, 1, n))


def _normalize(x):
    mu = jnp.mean(x, axis=-1, keepdims=True)
    xc = x - mu
    var = jnp.mean(xc * xc, axis=-1, keepdims=True)
    return xc * lax.rsqrt(var + LN_EPS)


def _rope128(x, cos, sin_a, sin_b):
    return (x * cos + pltpu.roll(x, LANES - AXIS_DIM // 2, 1) * sin_a
            + pltpu.roll(x, AXIS_DIM // 2, 1) * sin_b)


def _mod_index(b, j, n_lat_tiles, n_batch):
    return jnp.where(j >= n_lat_tiles, n_batch, b)


def _inproj_kernel(x_ref, mod_ref, w_ref, cos_ref, sa_ref, sb_ref,
                   u_ref, g_ref, q_ref, k_ref, v_ref):
    h = _normalize(x_ref[0]) * (1.0 + mod_ref[0, 1:2, :]) + mod_ref[0, 0:1, :]
    y = _dot(h.astype(_BF16), w_ref[...])
    u_ref[0] = y[:, :D_RNN]
    g_ref[0] = y[:, D_RNN:2 * D_RNN]
    cos, sa, sb = cos_ref[...], sa_ref[...], sb_ref[...]
    q0 = 2 * D_RNN
    for c in range(D_ATTN // LANES):
        qc = _rope128(y[:, q0 + c * LANES:q0 + (c + 1) * LANES], cos, sa, sb)
        q_ref[0, :, c * LANES:(c + 1) * LANES] = (qc * ATTN_SCALE).astype(_BF16)
    k0 = q0 + D_ATTN
    k_ref[0] = _rope128(y[:, k0:k0 + D_KV], cos, sa, sb).astype(_BF16)
    v_ref[0] = y[:, k0 + D_KV:k0 + 2 * D_KV].astype(_BF16)


def _inproj(x, mods, w_in, rope, n_lat_tiles):
    nb, r, d = x.shape
    tm = ROW_TILE
    nj = r // tm
    cos, sa, sb = rope
    rope_spec = pl.BlockSpec((tm, LANES), lambda b, j: (j, 0))
    row = lambda w: pl.BlockSpec((1, tm, w), lambda b, j: (b, j, 0))
    return pl.pallas_call(
        _inproj_kernel,
        grid=(nb, nj),
        in_specs=[
            row(d),
            pl.BlockSpec((1, 6, d), lambda b, j: (_mod_index(b, j, n_lat_tiles, nb), 0, 0)),
            pl.BlockSpec((d, D_IN), lambda b, j: (0, 0)),
            rope_spec, rope_spec, rope_spec,
        ],
        out_specs=[row(D_RNN), row(D_RNN), row(D_ATTN), row(D_KV), row(D_KV)],
        out_shape=[
            jax.ShapeDtypeStruct((nb, r, D_RNN), _F32),
            jax.ShapeDtypeStruct((nb, r, D_RNN), _F32),
            jax.ShapeDtypeStruct((nb, r, D_ATTN), _BF16),
            jax.ShapeDtypeStruct((nb, r, D_KV), _BF16),
            jax.ShapeDtypeStruct((nb, r, D_KV), _BF16),
        ],
        compiler_params=_cparams("arbitrary", "arbitrary"),
        name="inproj",
    )(x, mods, w_in, cos, sa, sb)


def _scan_block(step, reverse, n_lat, n_ctx):
    in_ctx = step < n_ctx
    i_ctx = (n_ctx - 1 - step) if reverse else step
    s_lat = step - n_ctx
    i_lat = (n_lat - 1 - s_lat) if reverse else s_lat
    blk = jnp.where(in_ctx, n_lat + i_ctx, i_lat)
    first = jnp.where(in_ctx, i_ctx == 0, i_lat == 0)
    last = jnp.where(in_ctx, i_ctx == n_ctx - 1, i_lat == n_lat - 1)
    return blk, first, last


def _rglru_kernel(reverse, n_lat, n_ctx, *refs):
    if reverse:
        (up_ref, uc_ref, un_ref, cw_ref, cb_ref, w_ref, bias_ref, lam_ref,
         hf_ref, g_ref, gn_ref, o_ref, ubuf, a_s, b_s, h_s, carry) = refs
    else:
        (up_ref, uc_ref, un_ref, cw_ref, cb_ref, w_ref, bias_ref, lam_ref,
         o_ref, ubuf, a_s, b_s, h_s, carry) = refs
    nb, ts, ch = uc_ref.shape
    step = pl.program_id(0)
    _, first, last = _scan_block(step, reverse, n_lat, n_ctx)

    @pl.when(step == 0)
    def _():
        carry[...] = jnp.zeros_like(carry)

    ubuf[:, SUBLANES:SUBLANES + ts, :] = uc_ref[...]
    ubuf[:, 0:SUBLANES, :] = jnp.where(first, 0.0, up_ref[...])
    ubuf[:, SUBLANES + ts:, :] = jnp.where(last, 0.0, un_ref[...])
    x = cb_ref[...][None]
    for j in range(CONV_W):
        off = SUBLANES + j - CONV_LEFT
        x = x + ubuf[:, off:off + ts, :] * cw_ref[j:j + 1, :][None]
    x = x.reshape(nb * ts, ch)

    gates = _sigmoid(_dot(x.astype(_BF16), w_ref[...]) + bias_ref[...])
    r = gates[:, :ch]
    gi = gates[:, ch:]
    neg_lam = -lam_ref[...]
    softplus = jnp.maximum(neg_lam, 0.0) + jnp.log1p(jnp.exp(-jnp.abs(neg_lam)))
    log_a = (-RG_C) * r * softplus
    a = jnp.exp(log_a)
    b = jnp.sqrt(jnp.tanh(-log_a) * (1.0 + a * a)) * (gi * x)
    n_chunk = ch // LANES
    for c in range(n_chunk):
        a_s[c] = a[:, c * LANES:(c + 1) * LANES]
        b_s[c] = b[:, c * LANES:(c + 1) * LANES]

    def body(i, hs):
        t = (ts - 1 - i) if reverse else i
        rows = pl.ds(t, nb, stride=ts)
        new = []
        for c in range(n_chunk):
            hc = a_s[c, rows, :] * hs[c] + b_s[c, rows, :]
            h_s[c, rows, :] = hc
            new.append(hc)
        return tuple(new)

    h0 = tuple(carry[c] for c in range(n_chunk))
    h_end = lax.fori_loop(0, ts, body, h0, unroll=8)
    for c in range(n_chunk):
        carry[c] = h_end[c]

    h = jnp.concatenate([h_s[c] for c in range(n_chunk)], axis=-1).reshape(nb, ts, ch)
    if reverse:
        g = g_ref[...]
        gelu = 0.5 * g * (1.0 + jnp.tanh(0.7978845608028654 * (g + 0.044715 * (g * g * g))))
        y = (hf_ref[...] + h) * gelu
        ms = jnp.mean(y * y, axis=-1, keepdims=True)
        o_ref[...] = (y * lax.rsqrt(ms + LN_EPS) * gn_ref[...][None]).astype(o_ref.dtype)
    else:
        o_ref[...] = h


def _rglru(u, conv_w, conv_b, w_gate, b_gate, lam, n_lat, n_ctx, reverse, hf=None, g=None, gn=None):
    nb, r, ch = u.shape
    ts = SCAN_TILE
    halo = ts // SUBLANES
    n_halo = r // SUBLANES

    def cur(s):
        return (0, _scan_block(s, reverse, n_lat, n_ctx)[0], 0)

    def prev(s):
        return (0, jnp.maximum(_scan_block(s, reverse, n_lat, n_ctx)[0] * halo - 1, 0), 0)

    def nxt(s):
        return (0, jnp.minimum((_scan_block(s, reverse, n_lat, n_ctx)[0] + 1) * halo, n_halo - 1), 0)

    full = lambda shape: pl.BlockSpec(shape, lambda s: (0,) * len(shape))
    blk = pl.BlockSpec((nb, ts, ch), cur)
    in_specs = [
        pl.BlockSpec((nb, SUBLANES, ch), prev), blk, pl.BlockSpec((nb, SUBLANES, ch), nxt),
        full((CONV_W, ch)), full((1, ch)), full((ch, 2 * ch)), full((1, 2 * ch)), full((1, ch)),
    ]
    args = [u, u, u, conv_w, conv_b, w_gate, b_gate, lam]
    if reverse:
        in_specs += [blk, blk, full((1, ch))]
        args += [hf, g, gn]
    return pl.pallas_call(
        functools.partial(_rglru_kernel, reverse, n_lat, n_ctx),
        grid=(n_lat + n_ctx,),
        in_specs=in_specs,
        out_specs=blk,
        out_shape=jax.ShapeDtypeStruct((nb, r, ch), _BF16 if reverse else _F32),
        scratch_shapes=[
            pltpu.VMEM((nb, ts + 2 * SUBLANES, ch), _F32),
            pltpu.VMEM((ch // LANES, nb * ts, LANES), _F32),
            pltpu.VMEM((ch // LANES, nb * ts, LANES), _F32),
            pltpu.VMEM((ch // LANES, nb * ts, LANES), _F32),
            pltpu.VMEM((ch // LANES, nb, LANES), _F32),
        ],
        compiler_params=_cparams("arbitrary"),
        name="rglru_bwd" if reverse else "rglru_fwd",
    )(*args)


def _attn_kernel(n_lat, sink_ref, q_ref, kp_ref, kc_ref, kn_ref, kx_ref,
                 vp_ref, vc_ref, vn_ref, vx_ref, gn_ref, o_ref):
    n = pl.program_id(1)
    n_ctx_keys = kx_ref.shape[1]
    n_keys = 3 * BLK + n_ctx_keys
    is_lat = n < n_lat
    lo = jnp.where(is_lat, jnp.where(n >= 1, 0, BLK), 3 * BLK)
    hi = jnp.where(is_lat, jnp.where(n + 1 < n_lat, 3 * BLK, 2 * BLK), 0)
    col = lax.broadcasted_iota(jnp.int32, (BLK, n_keys), 1)
    row = lax.broadcasted_iota(jnp.int32, (BLK, n_keys), 0)
    in_win = jnp.abs(col - BLK - row) <= WINDOW
    mask = ((col >= lo) & (col < hi) & in_win) | (col >= 3 * BLK)

    q = q_ref[0]
    k_all = jnp.concatenate([kp_ref[0], kc_ref[0], kn_ref[0], kx_ref[0]], axis=0)
    v_all = jnp.concatenate([vp_ref[0], vc_ref[0], vn_ref[0], vx_ref[0]], axis=0)
    outs = []
    for hq in range(N_Q_HEADS):
        hk = hq // GQ
        qh = q[:, hq * HEAD_DIM:(hq + 1) * HEAD_DIM]
        kh = k_all[:, hk * HEAD_DIM:(hk + 1) * HEAD_DIM]
        vh = v_all[:, hk * HEAD_DIM:(hk + 1) * HEAD_DIM]
        s = lax.dot_general(qh, kh, (((1,), (1,)), ((), ())), preferred_element_type=_F32)
        s = jnp.where(mask, s, NEG_INF)
        sink = sink_ref[hq]
        m = jnp.maximum(jnp.max(s, axis=-1, keepdims=True), sink)
        p = jnp.exp(s - m)
        denom = jnp.sum(p, axis=-1, keepdims=True) + jnp.exp(sink - m)
        outs.append(_dot(p.astype(_BF16), vh) / denom)
    o = jnp.concatenate(outs, axis=-1)
    ms = jnp.mean(o * o, axis=-1, keepdims=True)
    o_ref[0] = (o * lax.rsqrt(ms + LN_EPS) * gn_ref[...]).astype(o_ref.dtype)


def _attention(q, k, v, sink, gn, n_lat, ctx_len):
    nb, r, _ = q.shape
    nq = r // BLK
    ctx_blk = (n_lat * BLK) // ctx_len
    band = lambda f: pl.BlockSpec((1, BLK, D_KV), lambda b, n: (b, f(n), 0))
    prev = band(lambda n: jnp.clip(n - 1, 0, n_lat - 1))
    cur = band(lambda n: jnp.minimum(n, n_lat - 1))
    nxt = band(lambda n: jnp.clip(n + 1, 0, n_lat - 1))
    ctx = pl.BlockSpec((1, ctx_len, D_KV), lambda b, n: (b, ctx_blk, 0))
    return pl.pallas_call(
        functools.partial(_attn_kernel, n_lat),
        grid=(nb, nq),
        in_specs=[
            pl.BlockSpec(memory_space=pltpu.SMEM),
            pl.BlockSpec((1, BLK, D_ATTN), lambda b, n: (b, n, 0)),
            prev, cur, nxt, ctx, prev, cur, nxt, ctx,
            pl.BlockSpec((1, D_ATTN), lambda b, n: (0, 0)),
        ],
        out_specs=pl.BlockSpec((1, BLK, D_ATTN), lambda b, n: (b, n, 0)),
        out_shape=jax.ShapeDtypeStruct((nb, r, D_ATTN), _BF16),
        compiler_params=_cparams("arbitrary", "arbitrary"),
        name="attention",
    )(sink, q, k, k, k, k, v, v, v, v, gn)


def _first_lane(cond, lane):
    return jnp.min(jnp.where(cond, lane, LOGIT_W), axis=-1, keepdims=True)


def _route_tile(logits, counts):
    tm = logits.shape[0]
    lane = lax.broadcasted_iota(jnp.int32, logits.shape, 1)
    is_group = lane < N_GROUPS
    g_max = jnp.max(jnp.where(is_group, logits, -jnp.inf), axis=-1, keepdims=True)
    g_sel = _first_lane(is_group & (logits == g_max), lane)
    p_g = 1.0 / jnp.sum(jnp.where(is_group, jnp.exp(logits - g_max), 0.0), axis=-1, keepdims=True)
    e_lo = N_GROUPS + EXP_PER_GROUP * g_sel
    in_grp = (lane >= e_lo) & (lane < e_lo + EXP_PER_GROUP)
    v1 = jnp.max(jnp.where(in_grp, logits, -jnp.inf), axis=-1, keepdims=True)
    i1 = _first_lane(in_grp & (logits == v1), lane)
    rest = in_grp & (lane != i1)
    v2 = jnp.max(jnp.where(rest, logits, -jnp.inf), axis=-1, keepdims=True)
    i2 = _first_lane(rest & (logits == v2), lane)
    e21 = jnp.exp(v2 - v1)
    gate1 = p_g / (1.0 + e21)
    gate2 = gate1 * e21
    eid1 = i1 - N_GROUPS
    eid2 = i2 - N_GROUPS
    oh1 = lane == eid1
    oh2 = lane == eid2
    onehot = jnp.where(oh1 | oh2, 1.0, 0.0)
    r_i = lax.broadcasted_iota(jnp.int32, (tm, tm), 0)
    c_i = lax.broadcasted_iota(jnp.int32, (tm, tm), 1)
    lower = jnp.where(c_i < r_i, 1.0, 0.0).astype(_BF16)
    before = _dot(lower, onehot.astype(_BF16)) + counts
    rank1 = jnp.sum(jnp.where(oh1, before, 0.0), axis=-1, keepdims=True)
    rank2 = jnp.sum(jnp.where(oh2, before, 0.0), axis=-1, keepdims=True)
    new_counts = counts + jnp.sum(onehot, axis=0, keepdims=True)
    fields = (eid1.astype(_F32), eid2.astype(_F32), rank1, rank2, gate1, gate2)
    slab = jnp.zeros(logits.shape, _F32)
    for i, f in enumerate(fields):
        slab = jnp.where(lane == i, f, slab)
    return slab, new_counts


def _outproj_kernel(alpha, rn_ref, an_ref, x_ref, mod_ref, wr_ref, wa_ref, lg_ref, lb_ref,
                    wrt_ref, brt_ref, x1_ref, h2_ref, rt_ref, cnt_ref, cnt_s):
    @pl.when((pl.program_id(0) == 0) & (pl.program_id(1) == 0))
    def _():
        cnt_s[...] = jnp.zeros_like(cnt_s)

    y = _dot(rn_ref[0], wr_ref[...]) + _dot(an_ref[0], wa_ref[...])
    z = alpha * x_ref[0] + mod_ref[0, 2:3, :] * y
    x1 = _normalize(z) * lg_ref[...] + lb_ref[...]
    x1_ref[0] = x1
    h2 = _normalize(x1) * (1.0 + mod_ref[0, 4:5, :]) + mod_ref[0, 3:4, :]
    h2_ref[0] = h2
    logits = _dot(h2.astype(_BF16), wrt_ref[...]) + brt_ref[...]
    slab, counts = _route_tile(logits, cnt_s[...])
    rt_ref[0] = slab
    cnt_s[...] = counts
    cnt_ref[...] = counts


def _outproj(rn, an, x, mods, w_out, ln_g, ln_b, w_rt, b_rt, n_lat_tiles, alpha):
    nb, r, d = x.shape
    tm = ROW_TILE
    row = lambda w: pl.BlockSpec((1, tm, w), lambda b, j: (b, j, 0))
    full = lambda shape: pl.BlockSpec(shape, lambda b, j: (0,) * len(shape))
    return pl.pallas_call(
        functools.partial(_outproj_kernel, alpha),
        grid=(nb, r // tm),
        in_specs=[
            row(D_RNN), row(D_ATTN), row(d),
            pl.BlockSpec((1, 6, d), lambda b, j: (_mod_index(b, j, n_lat_tiles, nb), 0, 0)),
            pl.BlockSpec((D_RNN, d), lambda b, j: (0, 0)),
            pl.BlockSpec((D_ATTN, d), lambda b, j: (1, 0)),
            full((1, d)), full((1, d)), full((d, LOGIT_W)), full((1, LOGIT_W)),
        ],
        out_specs=[row(d), row(d), row(LOGIT_W), full((1, LOGIT_W))],
        out_shape=[
            jax.ShapeDtypeStruct((nb, r, d), _F32),
            jax.ShapeDtypeStruct((nb, r, d), _F32),
            jax.ShapeDtypeStruct((nb, r, LOGIT_W), _F32),
            jax.ShapeDtypeStruct((1, LOGIT_W), _F32),
        ],
        scratch_shapes=[pltpu.VMEM((1, LOGIT_W), _F32)],
        compiler_params=_cparams("arbitrary", "arbitrary"),
        name="outproj",
    )(rn, an, x, mods, w_out, w_out, ln_g, ln_b, w_rt, b_rt)


def _row_gather(idx_ref, n_rows, src_hbm, dst, sem):
    def body(i, carry):
        t = idx_ref[0, 0, i]
        pltpu.make_async_copy(src_hbm.at[pl.ds(t, 1), :], dst.at[pl.ds(i, 1), :], sem).start()
        return carry

    lax.fori_loop(0, n_rows, body, 0, unroll=8)


def _moe_kernel(blk_e_ref, tok_ref, tok_next_ref, h_hbm, w1_ref, w3_ref, w2_ref, o_ref,
                xbuf, sem, w1_s, w3_s, w2_s):
    j = pl.program_id(0)
    n = pl.num_programs(0)
    slot = j % 2
    bm = xbuf.shape[1]

    @pl.when(j == 0)
    def _():
        _row_gather(tok_ref, bm, h_hbm, xbuf.at[0], sem.at[0])

    @pl.when(j + 1 < n)
    def _():
        _row_gather(tok_next_ref, bm, h_hbm, xbuf.at[1 - slot], sem.at[1 - slot])

    @pl.when((j == 0) | (blk_e_ref[j] != blk_e_ref[jnp.maximum(j - 1, 0)]))
    def _():
        w1_s[...] = w1_ref[0].astype(_BF16)
        w3_s[...] = w3_ref[0].astype(_BF16)
        w2_s[...] = w2_ref[0].astype(_BF16)

    pltpu.make_async_copy(h_hbm.at[pl.ds(0, bm), :], xbuf.at[slot], sem.at[slot]).wait()
    x = xbuf[slot].astype(_BF16)
    a = _dot(x, w1_s[...])
    h = (a * _sigmoid(a)) * _dot(x, w3_s[...])
    o_ref[...] = _dot(h.astype(_BF16), w2_s[...])


def _moe_ffn(h2, slot_tok, blk_e, w1, w3, w2):
    nt, d = h2.shape
    bm = MOE_TILE
    n_blk = blk_e.shape[0]
    tok3 = slot_tok.reshape(n_blk, 1, bm)
    smem_blk = lambda f: pl.BlockSpec((1, 1, bm), lambda j, be: (f(j), 0, 0), memory_space=pltpu.SMEM)
    grid_spec = pltpu.PrefetchScalarGridSpec(
        num_scalar_prefetch=1,
        grid=(n_blk,),
        in_specs=[
            smem_blk(lambda j: j),
            smem_blk(lambda j: jnp.minimum(j + 1, n_blk - 1)),
            pl.BlockSpec(memory_space=pl.ANY),
            pl.BlockSpec((1, d, D_EXPERT), lambda j, be: (be[j], 0, 0)),
            pl.BlockSpec((1, d, D_EXPERT), lambda j, be: (be[j], 0, 0)),
            pl.BlockSpec((1, D_EXPERT, d), lambda j, be: (be[j], 0, 0)),
        ],
        out_specs=pl.BlockSpec((bm, d), lambda j, be: (j, 0)),
        scratch_shapes=[
            pltpu.VMEM((2, bm, d), _F32), pltpu.SemaphoreType.DMA((2,)),
            pltpu.VMEM((d, D_EXPERT), _BF16), pltpu.VMEM((d, D_EXPERT), _BF16),
            pltpu.VMEM((D_EXPERT, d), _BF16),
        ],
    )
    return pl.pallas_call(
        _moe_kernel,
        grid_spec=grid_spec,
        out_shape=jax.ShapeDtypeStruct((n_blk * bm, d), _F32),
        compiler_params=_cparams("arbitrary"),
        name="moe_ffn",
    )(blk_e, tok3, tok3, h2, w1, w3, w2)


def _combine_kernel(alpha, dst_ref, dst_next_ref, y_hbm, gate_ref, x_ref, mod_ref, lg_ref, lb_ref,
                    o_ref, ybuf, sem):
    s = pl.program_id(0)
    n = pl.num_programs(0)
    slot = s % 2
    tm = x_ref.shape[0]

    @pl.when(s == 0)
    def _():
        _row_gather(dst_ref, TOP_K * tm, y_hbm, ybuf.at[0], sem.at[0])

    @pl.when(s + 1 < n)
    def _():
        _row_gather(dst_next_ref, TOP_K * tm, y_hbm, ybuf.at[1 - slot], sem.at[1 - slot])

    pltpu.make_async_copy(y_hbm.at[pl.ds(0, TOP_K * tm), :], ybuf.at[slot], sem.at[slot]).wait()
    gate = gate_ref[...]
    y = ybuf[slot, 0:tm, :] * gate[:, 0:1] + ybuf[slot, tm:2 * tm, :] * gate[:, 1:2]
    z = alpha * x_ref[...] + mod_ref[0, 5:6, :] * y
    o_ref[...] = _normalize(z) * lg_ref[...] + lb_ref[...]


def _combine(yb, dest, gate, x1, mods, ln_g, ln_b, n_batch, tiles_per_batch, tiles_used, alpha):
    nt, d = x1.shape
    tm = ROW_TILE
    n_steps = n_batch * tiles_used
    dst3 = dest.reshape(nt // tm, tm, TOP_K).transpose(0, 2, 1).reshape(nt // tm, 1, TOP_K * tm)

    def tile(s):
        return (s // tiles_used) * tiles_per_batch + s % tiles_used

    def mod_idx(s):
        return (jnp.where(s % tiles_used >= tiles_per_batch - 1, n_batch, s // tiles_used), 0, 0)

    smem_blk = lambda f: pl.BlockSpec((1, 1, TOP_K * tm), lambda s: (tile(f(s)), 0, 0),
                                      memory_space=pltpu.SMEM)
    full = lambda shape: pl.BlockSpec(shape, lambda s: (0,) * len(shape))
    return pl.pallas_call(
        functools.partial(_combine_kernel, alpha),
        grid=(n_steps,),
        in_specs=[
            smem_blk(lambda s: s),
            smem_blk(lambda s: jnp.minimum(s + 1, n_steps - 1)),
            pl.BlockSpec(memory_space=pl.ANY),
            pl.BlockSpec((tm, TOP_K), lambda s: (tile(s), 0)),
            pl.BlockSpec((tm, d), lambda s: (tile(s), 0)),
            pl.BlockSpec((1, 6, d), mod_idx),
            full((1, d)), full((1, d)),
        ],
        out_specs=pl.BlockSpec((tm, d), lambda s: (s, 0)),
        out_shape=jax.ShapeDtypeStruct((n_steps * tm, d), _F32),
        scratch_shapes=[pltpu.VMEM((2, TOP_K * tm, d), _F32), pltpu.SemaphoreType.DMA((2,))],
        compiler_params=_cparams("arbitrary"),
        name="combine_ln2",
    )(dst3, dst3, yb, gate, x1, mods, ln_g, ln_b)


def _route(slab, counts, n_blk):
    nt = slab.shape[0]
    bm = MOE_TILE
    eid = slab[:, 0:TOP_K].astype(jnp.int32)
    rank = slab[:, TOP_K:2 * TOP_K].astype(jnp.int32)
    gate = slab[:, 2 * TOP_K:3 * TOP_K]
    counts = counts[0, :N_EXPERTS].astype(jnp.int32)
    experts = jnp.arange(N_EXPERTS, dtype=jnp.int32)
    pcounts = (counts + bm - 1) // bm * bm
    pends = jnp.cumsum(pcounts)
    pstarts = pends - pcounts
    dest = jnp.sum(jnp.where(eid[:, :, None] == experts, pstarts, 0), axis=-1) + rank
    n_assign = nt * TOP_K
    n_fill = n_blk * bm - n_assign
    pad_ends = jnp.cumsum(pcounts - counts)
    fill_key = jnp.sum(pad_ends[None, :] <= jnp.arange(n_fill, dtype=jnp.int32)[:, None], axis=-1,
                       dtype=jnp.int32)
    keys = jnp.concatenate([eid.reshape(-1), fill_key])
    vals = jnp.concatenate([jnp.arange(n_assign, dtype=jnp.int32) // TOP_K, jnp.zeros((n_fill,), jnp.int32)])
    _, slot_tok = lax.sort_key_val(keys, vals)
    blk_start = jnp.arange(n_blk, dtype=jnp.int32) * bm
    blk_e = jnp.minimum(jnp.sum(pends[None, :] <= blk_start[:, None], axis=-1, dtype=jnp.int32), N_EXPERTS - 1)
    return dest, gate, slot_tok, blk_e


def _rope_tables(seq, ident_rows):
    rows = seq // GRID_W
    row = jnp.repeat(jnp.arange(rows, dtype=_F32), GRID_W)
    col = jnp.tile(jnp.arange(GRID_W, dtype=_F32), rows)
    inv = ROPE_BASE ** (-jnp.arange(0, AXIS_DIM, 2, dtype=_F32) / AXIS_DIM)
    ang_r = row[:, None] * inv
    ang_c = col[:, None] * inv
    cr, sr, cc, sc = jnp.cos(ang_r), jnp.sin(ang_r), jnp.cos(ang_c), jnp.sin(ang_c)
    z = jnp.zeros_like(sr)
    reps = LANES // HEAD_DIM
    cos = jnp.tile(jnp.concatenate([cr, cr, cc, cc], -1), (1, reps))
    sin_a = jnp.tile(jnp.concatenate([-sr, z, -sc, z], -1), (1, reps))
    sin_b = jnp.tile(jnp.concatenate([z, sr, z, sc], -1), (1, reps))
    one = jnp.ones((ident_rows, LANES), _F32)
    zero = jnp.zeros((ident_rows, LANES), _F32)
    return (jnp.concatenate([cos, one], 0), jnp.concatenate([sin_a, zero], 0),
            jnp.concatenate([sin_b, zero], 0))


def _block_diag(w):
    eye = jnp.eye(RG_BLOCKS, dtype=w.dtype)
    return jnp.einsum("hij,hg->higj", w, eye).reshape(D_RNN, D_RNN)


def kernel(x, c, ctx, c_ctx, w_mod, b_mod, w_in, conv_w, conv_b, rg_wa, rg_ba, rg_wx, rg_bx, rg_lam,
           attn_sink, gn_rnn, gn_attn, w_out, ln1_g, ln1_b, router_g, router_gb, router_e, router_eb,
           w1, w3, w2, ln2_g, ln2_b):
    nb, seq, d = x.shape
    ctx_len = ctx.shape[1]
    depth = w_mod.shape[0]
    r = seq + ctx_len
    nt = nb * r
    alpha = (2 * depth) ** 0.25
    assert d == D_MODEL and seq % ROW_TILE == 0 and ctx_len % ROW_TILE == 0
    assert seq % GRID_W == 0 and seq % ctx_len == 0 and ctx_len % BLK == 0
    n_lat_tiles = seq // ROW_TILE
    tiles_per_batch = r // ROW_TILE
    assert tiles_per_batch == n_lat_tiles + 1, "one context tile per batch"
    n_blk = (nt * TOP_K + N_EXPERTS * (MOE_TILE - 1) + MOE_TILE - 1) // MOE_TILE

    pad_rows = (-(nb + 1)) % SUBLANES
    c_all = jnp.concatenate([c, c_ctx[None, :], jnp.zeros((pad_rows, d), _F32)], 0)
    mods_all = _modulation(c_all, w_mod, b_mod).reshape(depth, nb + 1 + pad_rows, 6, d)

    rope = _rope_tables(seq, ROW_TILE)
    xs = jnp.concatenate([x, ctx], axis=1)

    out = None
    for l in range(depth):
        last = l == depth - 1
        mods = mods_all[l]
        u, g, q, k, v = _inproj(xs, mods, w_in[l].astype(_BF16), rope, n_lat_tiles)

        n_lat_s, n_ctx_s = seq // SCAN_TILE, ctx_len // SCAN_TILE
        hf = None
        for dr in range(2):
            w_gate = jnp.concatenate([_block_diag(rg_wa[l, dr]), _block_diag(rg_wx[l, dr])], 1).astype(_BF16)
            b_gate = jnp.concatenate([rg_ba[l, dr], rg_bx[l, dr]])[None, :]
            args = (u, conv_w[l], conv_b[l][None, :], w_gate, b_gate, rg_lam[l, dr][None, :], n_lat_s, n_ctx_s)
            if dr == 0:
                hf = _rglru(*args, reverse=False)
            else:
                rn = _rglru(*args, reverse=True, hf=hf, g=g, gn=gn_rnn[l][None, :])

        an = _attention(q, k, v, attn_sink[l], gn_attn[l][None, :], seq // BLK, ctx_len)

        w_rt = jnp.concatenate([router_g[l], router_e[l],
                                jnp.zeros((d, LOGIT_W - N_GROUPS - N_EXPERTS), _F32)], 1).astype(_BF16)
        b_rt = jnp.concatenate([router_gb[l], router_eb[l],
                                jnp.zeros((LOGIT_W - N_GROUPS - N_EXPERTS,), _F32)])[None, :]
        x1, h2, slab, counts = _outproj(rn, an, xs, mods, w_out[l].astype(_BF16), ln1_g[l][None, :],
                                        ln1_b[l][None, :], w_rt, b_rt, n_lat_tiles, alpha)

        dest, gate, slot_tok, blk_e = _route(slab.reshape(nt, LOGIT_W), counts, n_blk)
        yb = _moe_ffn(h2.reshape(nt, d), slot_tok, blk_e, w1[l], w3[l], w2[l])
        tiles_used = n_lat_tiles if last else tiles_per_batch
        x2 = _combine(yb, dest, gate, x1.reshape(nt, d), mods, ln2_g[l][None, :], ln2_b[l][None, :],
                      nb, tiles_per_batch, tiles_used, alpha)
        if last:
            out = x2.reshape(nb, seq, d)
        else:
            xs = x2.reshape(nb, r, d)
    return out
```

```python
import functools

import jax
import jax.numpy as jnp
from jax import lax
from jax.experimental import pallas as pl
from jax.experimental.pallas import tpu as pltpu

D_MODEL = 1024
D_RNN = 512
D_ATTN = 512
HEAD_DIM = 64
N_Q_HEADS = 8
N_KV_HEADS = 2
GQ = N_Q_HEADS // N_KV_HEADS
D_KV = N_KV_HEADS * HEAD_DIM
WINDOW = 128
BLK = 128
ATTN_SCALE = HEAD_DIM ** -0.5
AXIS_DIM = HEAD_DIM // 2
ROPE_BASE = 10000.0
GRID_W = 64
RG_BLOCKS = 8
RG_BD = D_RNN // RG_BLOCKS
RG_C = 8.0
CONV_W = 4
CONV_LEFT = 2
N_GROUPS = 4
EXP_PER_GROUP = 8
N_EXPERTS = N_GROUPS * EXP_PER_GROUP
TOP_K = 2
D_EXPERT = 512
D_IN = 2 * D_RNN + D_ATTN + 2 * D_KV
LN_EPS = 1e-6
NEG_INF = -1e30

LANES = 128
SUBLANES = 8
ROW_TILE = 256
SCAN_TILE = 128
MOE_TILE = 256
LOGIT_W = LANES
VMEM_LIMIT = 56 * 1024 * 1024

_F32 = jnp.float32
_BF16 = jnp.bfloat16


def _cparams(*sem):
    return pltpu.CompilerParams(dimension_semantics=sem, vmem_limit_bytes=VMEM_LIMIT)


def _sigmoid(x):
    return 1.0 / (1.0 + jnp.exp(-x))


def _dot(a, b):
    return jnp.dot(a, b, preferred_element_type=_F32)


def _mod_kernel(c_ref, w_ref, b_ref, o_ref):
    c = c_ref[...]
    a = c * _sigmoid(c)
    a_hi = a.astype(_BF16)
    a_lo = (a - a_hi.astype(_F32)).astype(_BF16)
    w = w_ref[0]
    w_hi = w.astype(_BF16)
    w_lo = (w - w_hi.astype(_F32)).astype(_BF16)
    o_ref[0] = _dot(a_hi, w_hi) + (_dot(a_lo, w_hi) + _dot(a_hi, w_lo)) + b_ref[0]


def _modulation(c_all, w_mod, b_mod):
    depth, d, n = w_mod.shape
    rows = c_all.shape[0]
    tn = 1536
    return pl.pallas_call(
        _mod_kernel,
        grid=(depth, n // tn),
        in_specs=[
            pl.BlockSpec((rows, d), lambda l, j: (0, 0)),
            pl.BlockSpec((1, d, tn), lambda l, j: (l, 0, j)),
            pl.BlockSpec((1, 1, tn), lambda l, j: (l, 0, j)),
        ],
        out_specs=pl.BlockSpec((1, rows, tn), lambda l, j: (l, 0, j)),
        out_shape=jax.ShapeDtypeStruct((depth, rows, n), _F32),
        compiler_params=_cparams("arbitrary", "arbitrary"),
        name="modulation",
    )(c_all, w_mod, b_mod.reshape(depth, 1, n))


def _normalize(x):
    mu = jnp.mean(x, axis=-1, keepdims=True)
    xc = x - mu
    var = jnp.mean(xc * xc, axis=-1, keepdims=True)
    return xc * lax.rsqrt(var + LN_EPS)


def _rope128(x, cos, sin_a, sin_b):
    return (x * cos + pltpu.roll(x, LANES - AXIS_DIM // 2, 1) * sin_a
            + pltpu.roll(x, AXIS_DIM // 2, 1) * sin_b)


ROW_TILE_SUB = D_MODEL // LANES


def _store_row_tiles(ref, base, x):
    rows = x.shape[0]
    for s in range(ROW_TILE_SUB):
        ref[pl.ds(base * ROW_TILE_SUB + s, rows, stride=ROW_TILE_SUB), :] = x[:, s * LANES:(s + 1) * LANES]


def _load_row_tiles(ref, base, rows):
    return jnp.concatenate(
        [ref[pl.ds(base * ROW_TILE_SUB + s, rows, stride=ROW_TILE_SUB), :] for s in range(ROW_TILE_SUB)],
        axis=-1)


def _mod_index(b, j, n_lat_tiles, n_batch):
    return jnp.where(j >= n_lat_tiles, n_batch, b)


def _inproj_kernel(x_ref, mod_ref, w_ref, cos_ref, sa_ref, sb_ref,
                   u_ref, g_ref, q_ref, kt_ref, v_ref):
    h = _normalize(x_ref[0]) * (1.0 + mod_ref[0, 1:2, :]) + mod_ref[0, 0:1, :]
    y = _dot(h.astype(_BF16), w_ref[...])
    u_ref[0] = y[:, :D_RNN]
    g_ref[0] = y[:, D_RNN:2 * D_RNN]
    cos, sa, sb = cos_ref[...], sa_ref[...], sb_ref[...]
    q0 = 2 * D_RNN
    for c in range(D_ATTN // LANES):
        qc = _rope128(y[:, q0 + c * LANES:q0 + (c + 1) * LANES], cos, sa, sb)
        q_ref[0, :, c * LANES:(c + 1) * LANES] = (qc * ATTN_SCALE).astype(_BF16)
    k0 = q0 + D_ATTN
    kt_ref[0] = _rope128(y[:, k0:k0 + D_KV], cos, sa, sb).T.astype(_BF16)
    v_ref[0] = y[:, k0 + D_KV:k0 + 2 * D_KV].astype(_BF16)


def _inproj(x, mods, w_in, rope, n_lat_tiles):
    nb, r, d = x.shape
    tm = ROW_TILE
    nj = r // tm
    cos, sa, sb = rope
    rope_spec = pl.BlockSpec((tm, LANES), lambda b, j: (j, 0))
    row = lambda w: pl.BlockSpec((1, tm, w), lambda b, j: (b, j, 0))
    return pl.pallas_call(
        _inproj_kernel,
        grid=(nb, nj),
        in_specs=[
            row(d),
            pl.BlockSpec((1, 6, d), lambda b, j: (_mod_index(b, j, n_lat_tiles, nb), 0, 0)),
            pl.BlockSpec((d, D_IN), lambda b, j: (0, 0)),
            rope_spec, rope_spec, rope_spec,
        ],
        out_specs=[row(D_RNN), row(D_RNN), row(D_ATTN),
                   pl.BlockSpec((1, D_KV, tm), lambda b, j: (b, 0, j)), row(D_KV)],
        out_shape=[
            jax.ShapeDtypeStruct((nb, r, D_RNN), _F32),
            jax.ShapeDtypeStruct((nb, r, D_RNN), _F32),
            jax.ShapeDtypeStruct((nb, r, D_ATTN), _BF16),
            jax.ShapeDtypeStruct((nb, D_KV, r), _BF16),
            jax.ShapeDtypeStruct((nb, r, D_KV), _BF16),
        ],
        compiler_params=_cparams("arbitrary", "arbitrary"),
        name="inproj",
    )(x, mods, w_in, cos, sa, sb)


def _scan_block(step, reverse, n_lat, n_ctx):
    in_ctx = step < n_ctx
    i_ctx = (n_ctx - 1 - step) if reverse else step
    s_lat = step - n_ctx
    i_lat = (n_lat - 1 - s_lat) if reverse else s_lat
    blk = jnp.where(in_ctx, n_lat + i_ctx, i_lat)
    first = jnp.where(in_ctx, i_ctx == 0, i_lat == 0)
    last = jnp.where(in_ctx, i_ctx == n_ctx - 1, i_lat == n_lat - 1)
    return blk, first, last


def _scan_direction(x, w_ref, bias_ref, lam_ref, a_s, b_s, h_s, carry, reverse):
    nb, ts, ch = x.shape
    stride = a_s.shape[1] // nb
    x2 = x.reshape(nb * ts, ch)
    z = _dot(x2.astype(_BF16), w_ref[...]) + bias_ref[...]
    gates = 0.5 * jnp.tanh(0.5 * z) + 0.5
    r = gates[:, :ch]
    gi = gates[:, ch:]
    neg_lam = -lam_ref[...]
    softplus = jnp.maximum(neg_lam, 0.0) + jnp.log1p(jnp.exp(-jnp.abs(neg_lam)))
    log_a = (-RG_C) * r * softplus
    a = jnp.exp(log_a)
    t2 = jnp.tanh(-log_a) * (1.0 + a * a)
    b = jnp.where(t2 > 0.0, t2 * lax.rsqrt(t2), 0.0) * (gi * x2)
    n_chunk = ch // LANES
    for c in range(n_chunk):
        for i in range(nb):
            a_s[c, i * stride:i * stride + ts, :] = a[i * ts:(i + 1) * ts, c * LANES:(c + 1) * LANES]
            b_s[c, i * stride:i * stride + ts, :] = b[i * ts:(i + 1) * ts, c * LANES:(c + 1) * LANES]

    def body(i, hs):
        t = (ts - 1 - i) if reverse else i
        rows = pl.ds(t, nb, stride=stride)
        new = []
        for c in range(n_chunk):
            hc = a_s[c, rows, :] * hs[c] + b_s[c, rows, :]
            h_s[c, rows, :] = hc
            new.append(hc)
        return tuple(new)

    h0 = tuple(carry[c] for c in range(n_chunk))
    h_end = lax.fori_loop(0, ts, body, h0, unroll=8)
    for c in range(n_chunk):
        carry[c] = h_end[c]
    return jnp.stack(
        [jnp.concatenate([h_s[c, i * stride:i * stride + ts, :] for c in range(n_chunk)], axis=-1)
         for i in range(nb)], axis=0)


def _rglru_fwd_kernel(n_lat, n_ctx, up_ref, uc_ref, un_ref, cw_ref, cb_ref, w_ref, bias_ref, lam_ref,
                      x_ref, h_ref, ubuf, a_s, b_s, h_s, carry):
    nb, ts, ch = uc_ref.shape
    step = pl.program_id(0)
    _, first, last = _scan_block(step, False, n_lat, n_ctx)

    @pl.when(step == 0)
    def _():
        carry[...] = jnp.zeros_like(carry)

    ubuf[:, SUBLANES:SUBLANES + ts, :] = uc_ref[...]
    ubuf[:, 0:SUBLANES, :] = jnp.where(first, 0.0, up_ref[...])
    ubuf[:, SUBLANES + ts:, :] = jnp.where(last, 0.0, un_ref[...])
    x = cb_ref[...][None]
    for j in range(CONV_W):
        off = SUBLANES + j - CONV_LEFT
        x = x + ubuf[:, off:off + ts, :] * cw_ref[j:j + 1, :][None]
    x_ref[...] = x
    h_ref[...] = _scan_direction(x, w_ref, bias_ref, lam_ref, a_s, b_s, h_s, carry, False)


def _rglru_bwd_kernel(x_ref, w_ref, bias_ref, lam_ref, hf_ref, g_ref, gn_ref, o_ref, a_s, b_s, h_s, carry):
    @pl.when(pl.program_id(0) == 0)
    def _():
        carry[...] = jnp.zeros_like(carry)

    h = _scan_direction(x_ref[...], w_ref, bias_ref, lam_ref, a_s, b_s, h_s, carry, True)
    g = g_ref[...]
    gelu = 0.5 * g * (1.0 + jnp.tanh(0.7978845608028654 * (g + 0.044715 * (g * g * g))))
    y = (hf_ref[...] + h) * gelu
    ms = jnp.mean(y * y, axis=-1, keepdims=True)
    o_ref[...] = (y * lax.rsqrt(ms + LN_EPS) * gn_ref[...][None]).astype(o_ref.dtype)


def _scan_scratch(nb, ts, ch):
    rows = nb * (ts + SUBLANES)
    return [pltpu.VMEM((ch // LANES, rows, LANES), _F32)] * 3 + [pltpu.VMEM((ch // LANES, nb, LANES), _F32)]


def _rglru_fwd(u, conv_w, conv_b, w_gate, b_gate, lam, n_lat, n_ctx):
    nb, r, ch = u.shape
    ts = SCAN_TILE
    halo = ts // SUBLANES
    n_halo = r // SUBLANES
    blk_of = lambda s: _scan_block(s, False, n_lat, n_ctx)[0]
    full = lambda shape: pl.BlockSpec(shape, lambda s: (0,) * len(shape))
    blk = pl.BlockSpec((nb, ts, ch), lambda s: (0, blk_of(s), 0))
    return pl.pallas_call(
        functools.partial(_rglru_fwd_kernel, n_lat, n_ctx),
        grid=(n_lat + n_ctx,),
        in_specs=[
            pl.BlockSpec((nb, SUBLANES, ch), lambda s: (0, jnp.maximum(blk_of(s) * halo - 1, 0), 0)),
            blk,
            pl.BlockSpec((nb, SUBLANES, ch), lambda s: (0, jnp.minimum((blk_of(s) + 1) * halo, n_halo - 1), 0)),
            full((CONV_W, ch)), full((1, ch)), full((ch, 2 * ch)), full((1, 2 * ch)), full((1, ch)),
        ],
        out_specs=[blk, blk],
        out_shape=[jax.ShapeDtypeStruct((nb, r, ch), _F32)] * 2,
        scratch_shapes=[pltpu.VMEM((nb, ts + 2 * SUBLANES, ch), _F32)] + _scan_scratch(nb, ts, ch),
        compiler_params=_cparams("arbitrary"),
        name="rglru_fwd",
    )(u, u, u, conv_w, conv_b, w_gate, b_gate, lam)


def _rglru_bwd(x, w_gate, b_gate, lam, hf, g, gn, n_lat, n_ctx):
    nb, r, ch = x.shape
    ts = SCAN_TILE
    full = lambda shape: pl.BlockSpec(shape, lambda s: (0,) * len(shape))
    blk = pl.BlockSpec((nb, ts, ch), lambda s: (0, _scan_block(s, True, n_lat, n_ctx)[0], 0))
    return pl.pallas_call(
        _rglru_bwd_kernel,
        grid=(n_lat + n_ctx,),
        in_specs=[blk, full((ch, 2 * ch)), full((1, 2 * ch)), full((1, ch)), blk, blk, full((1, ch))],
        out_specs=blk,
        out_shape=jax.ShapeDtypeStruct((nb, r, ch), _BF16),
        scratch_shapes=_scan_scratch(nb, ts, ch),
        compiler_params=_cparams("arbitrary"),
        name="rglru_bwd",
    )(x, w_gate, b_gate, lam, hf, g, gn)


def _attn_kernel(n_lat, sink_ref, q_ref, kp_ref, kc_ref, kn_ref, kx_ref,
                 vp_ref, vc_ref, vn_ref, vx_ref, gn_ref, o_ref):
    n = pl.program_id(1)
    n_ctx_keys = kx_ref.shape[2]
    n_keys = 3 * BLK + n_ctx_keys
    is_lat = n < n_lat
    lo = jnp.where(is_lat, jnp.where(n >= 1, 0, BLK), 3 * BLK)
    hi = jnp.where(is_lat, jnp.where(n + 1 < n_lat, 3 * BLK, 2 * BLK), 0)
    col = lax.broadcasted_iota(jnp.int32, (BLK, n_keys), 1)
    row = lax.broadcasted_iota(jnp.int32, (BLK, n_keys), 0)
    in_win = jnp.abs(col - BLK - row) <= WINDOW
    mask = ((col >= lo) & (col < hi) & in_win) | (col >= 3 * BLK)

    lane1 = lax.broadcasted_iota(jnp.int32, (1, LANES), 1)
    keep_lo = jnp.where(lane1 < HEAD_DIM, 1.0, 0.0).astype(_BF16)
    keep_hi = jnp.where(lane1 >= HEAD_DIM, 1.0, 0.0).astype(_BF16)
    lane_o = lax.broadcasted_iota(jnp.int32, (BLK, LANES), 1)
    q = q_ref[0]
    kt_all = jnp.concatenate([kp_ref[0], kc_ref[0], kn_ref[0], kx_ref[0]], axis=1)
    v_all = jnp.concatenate([vp_ref[0], vc_ref[0], vn_ref[0], vx_ref[0]], axis=0)
    chunks = []
    for hk in range(N_KV_HEADS):
        kt_h = kt_all[hk * HEAD_DIM:(hk + 1) * HEAD_DIM, :]
        kt2 = jnp.concatenate([kt_h, kt_h], axis=0)
        qs = []
        for cc in range(GQ // 2):
            c = hk * (GQ // 2) + cc
            qc = q[:, c * LANES:(c + 1) * LANES]
            qs += [qc * keep_lo, qc * keep_hi]
        s_all = _dot(jnp.concatenate(qs, axis=0), kt2)
        ps, invs = [], []
        for g in range(GQ):
            s = jnp.where(mask, s_all[g * BLK:(g + 1) * BLK], NEG_INF)
            sink = sink_ref[hk * GQ + g]
            m = jnp.maximum(jnp.max(s, axis=-1, keepdims=True), sink)
            p = jnp.exp(s - m)
            invs.append(1.0 / (jnp.sum(p, axis=-1, keepdims=True) + jnp.exp(sink - m)))
            ps.append(p.astype(_BF16))
        o_all = _dot(jnp.concatenate(ps, axis=0), v_all)
        for cc in range(GQ // 2):
            even = o_all[(2 * cc) * BLK:(2 * cc + 1) * BLK] * invs[2 * cc]
            odd = o_all[(2 * cc + 1) * BLK:(2 * cc + 2) * BLK] * invs[2 * cc + 1]
            if hk == 0:
                odd = pltpu.roll(odd, HEAD_DIM, 1)
            else:
                even = pltpu.roll(even, HEAD_DIM, 1)
            chunks.append(jnp.where(lane_o < HEAD_DIM, even, odd))
    ssq = sum(jnp.sum(ch * ch, axis=-1, keepdims=True) for ch in chunks)
    scale = lax.rsqrt(ssq * (1.0 / D_ATTN) + LN_EPS)
    for c, ch in enumerate(chunks):
        o_ref[0, :, c * LANES:(c + 1) * LANES] = (
            ch * scale * gn_ref[:, c * LANES:(c + 1) * LANES]).astype(o_ref.dtype)


def _attention(q, kt, v, sink, gn, n_lat, ctx_len):
    nb, r, _ = q.shape
    nq = r // BLK
    ctx_blk = (n_lat * BLK) // ctx_len
    i_prev = lambda n: jnp.clip(n - 1, 0, n_lat - 1)
    i_cur = lambda n: jnp.minimum(n, n_lat - 1)
    i_next = lambda n: jnp.clip(n + 1, 0, n_lat - 1)
    band = lambda f: pl.BlockSpec((1, BLK, D_KV), lambda b, n: (b, f(n), 0))
    band_t = lambda f: pl.BlockSpec((1, D_KV, BLK), lambda b, n: (b, 0, f(n)))
    prev, cur, nxt = band(i_prev), band(i_cur), band(i_next)
    ctx = pl.BlockSpec((1, ctx_len, D_KV), lambda b, n: (b, ctx_blk, 0))
    ctx_t = pl.BlockSpec((1, D_KV, ctx_len), lambda b, n: (b, 0, ctx_blk))
    return pl.pallas_call(
        functools.partial(_attn_kernel, n_lat),
        grid=(nb, nq),
        in_specs=[
            pl.BlockSpec(memory_space=pltpu.SMEM),
            pl.BlockSpec((1, BLK, D_ATTN), lambda b, n: (b, n, 0)),
            band_t(i_prev), band_t(i_cur), band_t(i_next), ctx_t, prev, cur, nxt, ctx,
            pl.BlockSpec((1, D_ATTN), lambda b, n: (0, 0)),
        ],
        out_specs=pl.BlockSpec((1, BLK, D_ATTN), lambda b, n: (b, n, 0)),
        out_shape=jax.ShapeDtypeStruct((nb, r, D_ATTN), _BF16),
        compiler_params=_cparams("arbitrary", "arbitrary"),
        name="attention",
    )(sink, q, kt, kt, kt, kt, v, v, v, v, gn)


def _first_lane(cond, lane):
    return jnp.min(jnp.where(cond, lane, LOGIT_W), axis=-1, keepdims=True)


def _route_tile(logits, counts):
    tm = logits.shape[0]
    lane = lax.broadcasted_iota(jnp.int32, logits.shape, 1)
    is_group = lane < N_GROUPS
    g_max = jnp.max(jnp.where(is_group, logits, -jnp.inf), axis=-1, keepdims=True)
    g_sel = _first_lane(is_group & (logits == g_max), lane)
    p_g = 1.0 / jnp.sum(jnp.where(is_group, jnp.exp(logits - g_max), 0.0), axis=-1, keepdims=True)
    e_lo = N_GROUPS + EXP_PER_GROUP * g_sel
    in_grp = (lane >= e_lo) & (lane < e_lo + EXP_PER_GROUP)
    v1 = jnp.max(jnp.where(in_grp, logits, -jnp.inf), axis=-1, keepdims=True)
    i1 = _first_lane(in_grp & (logits == v1), lane)
    rest = in_grp & (lane != i1)
    v2 = jnp.max(jnp.where(rest, logits, -jnp.inf), axis=-1, keepdims=True)
    i2 = _first_lane(rest & (logits == v2), lane)
    e21 = jnp.exp(v2 - v1)
    gate1 = p_g / (1.0 + e21)
    gate2 = gate1 * e21
    eid1 = i1 - N_GROUPS
    eid2 = i2 - N_GROUPS
    oh1 = lane == eid1
    oh2 = lane == eid2
    onehot = jnp.where(oh1 | oh2, 1.0, 0.0)
    r_i = lax.broadcasted_iota(jnp.int32, (tm, tm), 0)
    c_i = lax.broadcasted_iota(jnp.int32, (tm, tm), 1)
    lower = jnp.where(c_i < r_i, 1.0, 0.0).astype(_BF16)
    before = _dot(lower, onehot.astype(_BF16)) + counts
    rank1 = jnp.sum(jnp.where(oh1, before, 0.0), axis=-1, keepdims=True)
    rank2 = jnp.sum(jnp.where(oh2, before, 0.0), axis=-1, keepdims=True)
    new_counts = counts + jnp.sum(onehot, axis=0, keepdims=True)
    fields = (eid1.astype(_F32), eid2.astype(_F32), rank1, rank2, gate1, gate2)
    slab = jnp.zeros(logits.shape, _F32)
    for i, f in enumerate(fields):
        slab = jnp.where(lane == i, f, slab)
    return slab, new_counts


def _outproj_kernel(alpha, rn_ref, an_ref, x_ref, mod_ref, wr_ref, wa_ref, lg_ref, lb_ref,
                    wrt_ref, brt_ref, x1_ref, h2_ref, rt_ref, cnt_ref, cnt_s):
    @pl.when((pl.program_id(0) == 0) & (pl.program_id(1) == 0))
    def _():
        cnt_s[...] = jnp.zeros_like(cnt_s)

    y = _dot(rn_ref[0], wr_ref[...]) + _dot(an_ref[0], wa_ref[...])
    z = alpha * x_ref[0] + mod_ref[0, 2:3, :] * y
    x1 = _normalize(z) * lg_ref[...] + lb_ref[...]
    x1_ref[0] = x1
    h2 = _normalize(x1) * (1.0 + mod_ref[0, 4:5, :]) + mod_ref[0, 3:4, :]
    _store_row_tiles(h2_ref, 0, h2)
    logits = _dot(h2.astype(_BF16), wrt_ref[...]) + brt_ref[...]
    slab, counts = _route_tile(logits, cnt_s[...])
    rt_ref[0] = slab
    cnt_s[...] = counts
    cnt_ref[...] = counts


def _outproj(rn, an, x, mods, w_out, ln_g, ln_b, w_rt, b_rt, n_lat_tiles, alpha):
    nb, r, d = x.shape
    tm = ROW_TILE
    row = lambda w: pl.BlockSpec((1, tm, w), lambda b, j: (b, j, 0))
    full = lambda shape: pl.BlockSpec(shape, lambda b, j: (0,) * len(shape))
    return pl.pallas_call(
        functools.partial(_outproj_kernel, alpha),
        grid=(nb, r // tm),
        in_specs=[
            row(D_RNN), row(D_ATTN), row(d),
            pl.BlockSpec((1, 6, d), lambda b, j: (_mod_index(b, j, n_lat_tiles, nb), 0, 0)),
            pl.BlockSpec((D_RNN, d), lambda b, j: (0, 0)),
            pl.BlockSpec((D_ATTN, d), lambda b, j: (1, 0)),
            full((1, d)), full((1, d)), full((d, LOGIT_W)), full((1, LOGIT_W)),
        ],
        out_specs=[row(d), pl.BlockSpec((tm * ROW_TILE_SUB, LANES), lambda b, j: (b * (r // tm) + j, 0)),
                   row(LOGIT_W), full((1, LOGIT_W))],
        out_shape=[
            jax.ShapeDtypeStruct((nb, r, d), _F32),
            jax.ShapeDtypeStruct((nb * r * ROW_TILE_SUB, LANES), _F32),
            jax.ShapeDtypeStruct((nb, r, LOGIT_W), _F32),
            jax.ShapeDtypeStruct((1, LOGIT_W), _F32),
        ],
        scratch_shapes=[pltpu.VMEM((1, LOGIT_W), _F32)],
        compiler_params=_cparams("arbitrary", "arbitrary"),
        name="outproj",
    )(rn, an, x, mods, w_out, w_out, ln_g, ln_b, w_rt, b_rt)


def _tile_rows(i):
    return pl.ds(pl.multiple_of(i * ROW_TILE_SUB, ROW_TILE_SUB), ROW_TILE_SUB)


def _row_gather(idx_ref, n_rows, src_hbm, dst, dst_base, sem):
    def body(i, carry):
        t = idx_ref[0, 0, i]
        pltpu.make_async_copy(src_hbm.at[_tile_rows(t), :], dst.at[_tile_rows(dst_base + i), :], sem).start()
        return carry

    lax.fori_loop(0, n_rows, body, 0, unroll=8)


def _gather_wait(n_rows, src_hbm, dst, dst_base, sem):
    pltpu.make_async_copy(src_hbm.at[pl.ds(0, n_rows * ROW_TILE_SUB), :],
                          dst.at[pl.ds(pl.multiple_of(dst_base * ROW_TILE_SUB, ROW_TILE_SUB),
                                       n_rows * ROW_TILE_SUB), :], sem).wait()


def _moe_kernel(blk_e_ref, tok_ref, tok_next_ref, h_hbm, w1_ref, w3_ref, w2_ref, o_ref,
                xbuf, sem, w1_s, w3_s, w2_s):
    j = pl.program_id(0)
    n = pl.num_programs(0)
    slot = j % 2
    bm = tok_ref.shape[2]

    @pl.when(j == 0)
    def _():
        _row_gather(tok_ref, bm, h_hbm, xbuf, 0, sem.at[0])

    @pl.when(j + 1 < n)
    def _():
        _row_gather(tok_next_ref, bm, h_hbm, xbuf, (1 - slot) * bm, sem.at[1 - slot])

    @pl.when((j == 0) | (blk_e_ref[j] != blk_e_ref[jnp.maximum(j - 1, 0)]))
    def _():
        w1_s[...] = w1_ref[0].astype(_BF16)
        w3_s[...] = w3_ref[0].astype(_BF16)
        w2_s[...] = w2_ref[0].astype(_BF16)

    _gather_wait(bm, h_hbm, xbuf, slot * bm, sem.at[slot])
    x = _load_row_tiles(xbuf, slot * bm, bm).astype(_BF16)
    a = _dot(x, w1_s[...])
    h = (a * _sigmoid(a)) * _dot(x, w3_s[...])
    _store_row_tiles(o_ref, 0, _dot(h.astype(_BF16), w2_s[...]))


def _moe_ffn(h2, slot_tok, blk_e, w1, w3, w2):
    d = D_MODEL
    bm = MOE_TILE
    n_blk = blk_e.shape[0]
    tok3 = slot_tok.reshape(n_blk, 1, bm)
    smem_blk = lambda f: pl.BlockSpec((1, 1, bm), lambda j, be: (f(j), 0, 0), memory_space=pltpu.SMEM)
    grid_spec = pltpu.PrefetchScalarGridSpec(
        num_scalar_prefetch=1,
        grid=(n_blk,),
        in_specs=[
            smem_blk(lambda j: j),
            smem_blk(lambda j: jnp.minimum(j + 1, n_blk - 1)),
            pl.BlockSpec(memory_space=pl.ANY),
            pl.BlockSpec((1, d, D_EXPERT), lambda j, be: (be[j], 0, 0)),
            pl.BlockSpec((1, d, D_EXPERT), lambda j, be: (be[j], 0, 0)),
            pl.BlockSpec((1, D_EXPERT, d), lambda j, be: (be[j], 0, 0)),
        ],
        out_specs=pl.BlockSpec((bm * ROW_TILE_SUB, LANES), lambda j, be: (j, 0)),
        scratch_shapes=[
            pltpu.VMEM((2 * bm * ROW_TILE_SUB, LANES), _F32), pltpu.SemaphoreType.DMA((2,)),
            pltpu.VMEM((d, D_EXPERT), _BF16), pltpu.VMEM((d, D_EXPERT), _BF16),
            pltpu.VMEM((D_EXPERT, d), _BF16),
        ],
    )
    return pl.pallas_call(
        _moe_kernel,
        grid_spec=grid_spec,
        out_shape=jax.ShapeDtypeStruct((n_blk * bm * ROW_TILE_SUB, LANES), _F32),
        compiler_params=_cparams("arbitrary"),
        name="moe_ffn",
    )(blk_e, tok3, tok3, h2, w1, w3, w2)


def _combine_kernel(alpha, dst_ref, dst_next_ref, y_hbm, gate_ref, x_ref, mod_ref, lg_ref, lb_ref,
                    o_ref, ybuf, sem):
    s = pl.program_id(0)
    n = pl.num_programs(0)
    slot = s % 2
    tm = x_ref.shape[0]

    nr = TOP_K * tm

    @pl.when(s == 0)
    def _():
        _row_gather(dst_ref, nr, y_hbm, ybuf, 0, sem.at[0])

    @pl.when(s + 1 < n)
    def _():
        _row_gather(dst_next_ref, nr, y_hbm, ybuf, (1 - slot) * nr, sem.at[1 - slot])

    _gather_wait(nr, y_hbm, ybuf, slot * nr, sem.at[slot])
    gate = gate_ref[...]
    y = (_load_row_tiles(ybuf, slot * nr, tm) * gate[:, 0:1]
         + _load_row_tiles(ybuf, slot * nr + tm, tm) * gate[:, 1:2])
    z = alpha * x_ref[...] + mod_ref[0, 5:6, :] * y
    o_ref[...] = _normalize(z) * lg_ref[...] + lb_ref[...]


def _combine(yb, dest, gate, x1, mods, ln_g, ln_b, n_batch, tiles_per_batch, tiles_used, alpha):
    nt, d = x1.shape
    tm = ROW_TILE
    n_steps = n_batch * tiles_used
    dst3 = dest.reshape(nt // tm, tm, TOP_K).transpose(0, 2, 1).reshape(nt // tm, 1, TOP_K * tm)

    def tile(s):
        return (s // tiles_used) * tiles_per_batch + s % tiles_used

    def mod_idx(s):
        return (jnp.where(s % tiles_used >= tiles_per_batch - 1, n_batch, s // tiles_used), 0, 0)

    smem_blk = lambda f: pl.BlockSpec((1, 1, TOP_K * tm), lambda s: (tile(f(s)), 0, 0),
                                      memory_space=pltpu.SMEM)
    full = lambda shape: pl.BlockSpec(shape, lambda s: (0,) * len(shape))
    return pl.pallas_call(
        functools.partial(_combine_kernel, alpha),
        grid=(n_steps,),
        in_specs=[
            smem_blk(lambda s: s),
            smem_blk(lambda s: jnp.minimum(s + 1, n_steps - 1)),
            pl.BlockSpec(memory_space=pl.ANY),
            pl.BlockSpec((tm, TOP_K), lambda s: (tile(s), 0)),
            pl.BlockSpec((tm, d), lambda s: (tile(s), 0)),
            pl.BlockSpec((1, 6, d), mod_idx),
            full((1, d)), full((1, d)),
        ],
        out_specs=pl.BlockSpec((tm, d), lambda s: (s, 0)),
        out_shape=jax.ShapeDtypeStruct((n_steps * tm, d), _F32),
        scratch_shapes=[pltpu.VMEM((2 * TOP_K * tm * ROW_TILE_SUB, LANES), _F32),
                        pltpu.SemaphoreType.DMA((2,))],
        compiler_params=_cparams("arbitrary"),
        name="combine_ln2",
    )(dst3, dst3, yb, gate, x1, mods, ln_g, ln_b)


def _route(slab, counts, n_blk):
    nt = slab.shape[0]
    bm = MOE_TILE
    eid = slab[:, 0:TOP_K].astype(jnp.int32)
    rank = slab[:, TOP_K:2 * TOP_K].astype(jnp.int32)
    gate = slab[:, 2 * TOP_K:3 * TOP_K]
    counts = counts[0, :N_EXPERTS].astype(jnp.int32)
    experts = jnp.arange(N_EXPERTS, dtype=jnp.int32)
    pcounts = (counts + bm - 1) // bm * bm
    pends = jnp.cumsum(pcounts)
    pstarts = pends - pcounts
    dest = jnp.sum(jnp.where(eid[:, :, None] == experts, pstarts, 0), axis=-1) + rank
    n_assign = nt * TOP_K
    n_fill = n_blk * bm - n_assign
    pad_ends = jnp.cumsum(pcounts - counts)
    fill_key = jnp.sum(pad_ends[None, :] <= jnp.arange(n_fill, dtype=jnp.int32)[:, None], axis=-1,
                       dtype=jnp.int32)
    keys = jnp.concatenate([eid.reshape(-1), fill_key])
    vals = jnp.concatenate([jnp.arange(n_assign, dtype=jnp.int32) // TOP_K, jnp.zeros((n_fill,), jnp.int32)])
    _, slot_tok = lax.sort_key_val(keys, vals)
    blk_start = jnp.arange(n_blk, dtype=jnp.int32) * bm
    blk_e = jnp.minimum(jnp.sum(pends[None, :] <= blk_start[:, None], axis=-1, dtype=jnp.int32), N_EXPERTS - 1)
    return dest, gate, slot_tok, blk_e


def _rope_tables(seq, ident_rows):
    rows = seq // GRID_W
    row = jnp.repeat(jnp.arange(rows, dtype=_F32), GRID_W)
    col = jnp.tile(jnp.arange(GRID_W, dtype=_F32), rows)
    inv = ROPE_BASE ** (-jnp.arange(0, AXIS_DIM, 2, dtype=_F32) / AXIS_DIM)
    ang_r = row[:, None] * inv
    ang_c = col[:, None] * inv
    cr, sr, cc, sc = jnp.cos(ang_r), jnp.sin(ang_r), jnp.cos(ang_c), jnp.sin(ang_c)
    z = jnp.zeros_like(sr)
    reps = LANES // HEAD_DIM
    cos = jnp.tile(jnp.concatenate([cr, cr, cc, cc], -1), (1, reps))
    sin_a = jnp.tile(jnp.concatenate([-sr, z, -sc, z], -1), (1, reps))
    sin_b = jnp.tile(jnp.concatenate([z, sr, z, sc], -1), (1, reps))
    one = jnp.ones((ident_rows, LANES), _F32)
    zero = jnp.zeros((ident_rows, LANES), _F32)
    return (jnp.concatenate([cos, one], 0), jnp.concatenate([sin_a, zero], 0),
            jnp.concatenate([sin_b, zero], 0))


def _block_diag(w):
    eye = jnp.eye(RG_BLOCKS, dtype=w.dtype)
    return jnp.einsum("hij,hg->higj", w, eye).reshape(D_RNN, D_RNN)


def kernel(x, c, ctx, c_ctx, w_mod, b_mod, w_in, conv_w, conv_b, rg_wa, rg_ba, rg_wx, rg_bx, rg_lam,
           attn_sink, gn_rnn, gn_attn, w_out, ln1_g, ln1_b, router_g, router_gb, router_e, router_eb,
           w1, w3, w2, ln2_g, ln2_b):
    nb, seq, d = x.shape
    ctx_len = ctx.shape[1]
    depth = w_mod.shape[0]
    r = seq + ctx_len
    nt = nb * r
    alpha = (2 * depth) ** 0.25
    assert d == D_MODEL and seq % ROW_TILE == 0 and ctx_len % ROW_TILE == 0
    assert seq % GRID_W == 0 and seq % ctx_len == 0 and ctx_len % BLK == 0
    n_lat_tiles = seq // ROW_TILE
    tiles_per_batch = r // ROW_TILE
    assert tiles_per_batch == n_lat_tiles + 1, "one context tile per batch"
    n_blk = (nt * TOP_K + N_EXPERTS * (MOE_TILE - 1) + MOE_TILE - 1) // MOE_TILE

    pad_rows = (-(nb + 1)) % SUBLANES
    c_all = jnp.concatenate([c, c_ctx[None, :], jnp.zeros((pad_rows, d), _F32)], 0)
    mods_all = _modulation(c_all, w_mod, b_mod).reshape(depth, nb + 1 + pad_rows, 6, d)

    rope = _rope_tables(seq, ROW_TILE)
    xs = jnp.concatenate([x, ctx], axis=1)

    out = None
    for l in range(depth):
        last = l == depth - 1
        mods = mods_all[l]
        u, g, q, k, v = _inproj(xs, mods, w_in[l].astype(_BF16), rope, n_lat_tiles)

        n_lat_s, n_ctx_s = seq // SCAN_TILE, ctx_len // SCAN_TILE
        gate_w = [jnp.concatenate([_block_diag(rg_wa[l, dr]), _block_diag(rg_wx[l, dr])], 1).astype(_BF16)
                  for dr in range(2)]
        gate_b = [jnp.concatenate([rg_ba[l, dr], rg_bx[l, dr]])[None, :] for dr in range(2)]
        xc, hf = _rglru_fwd(u, conv_w[l], conv_b[l][None, :], gate_w[0], gate_b[0], rg_lam[l, 0][None, :],
                            n_lat_s, n_ctx_s)
        rn = _rglru_bwd(xc, gate_w[1], gate_b[1], rg_lam[l, 1][None, :], hf, g, gn_rnn[l][None, :],
                        n_lat_s, n_ctx_s)

        an = _attention(q, k, v, attn_sink[l], gn_attn[l][None, :], seq // BLK, ctx_len)

        w_rt = jnp.concatenate([router_g[l], router_e[l],
                                jnp.zeros((d, LOGIT_W - N_GROUPS - N_EXPERTS), _F32)], 1).astype(_BF16)
        b_rt = jnp.concatenate([router_gb[l], router_eb[l],
                                jnp.zeros((LOGIT_W - N_GROUPS - N_EXPERTS,), _F32)])[None, :]
        x1, h2, slab, counts = _outproj(rn, an, xs, mods, w_out[l].astype(_BF16), ln1_g[l][None, :],
                                        ln1_b[l][None, :], w_rt, b_rt, n_lat_tiles, alpha)

        dest, gate, slot_tok, blk_e = _route(slab.reshape(nt, LOGIT_W), counts, n_blk)
        yb = _moe_ffn(h2, slot_tok, blk_e, w1[l], w3[l], w2[l])
        tiles_used = n_lat_tiles if last else tiles_per_batch
        x2 = _combine(yb, dest, gate, x1.reshape(nt, d), mods, ln2_g[l][None, :], ln2_b[l][None, :],
                      nb, tiles_per_batch, tiles_used, alpha)
        if last:
            out = x2.reshape(nb, seq, d)
        else:
            xs = x2.reshape(nb, r, d)
    return out
```

```python
import functools

import jax
import jax.numpy as jnp
from jax import lax
from jax.experimental import pallas as pl
from jax.experimental.pallas import tpu as pltpu

D_MODEL = 1024
D_RNN = 512
D_ATTN = 512
HEAD_DIM = 64
N_Q_HEADS = 8
N_KV_HEADS = 2
GQ = N_Q_HEADS // N_KV_HEADS
D_KV = N_KV_HEADS * HEAD_DIM
WINDOW = 128
BLK = 128
ATTN_SCALE = HEAD_DIM ** -0.5
AXIS_DIM = HEAD_DIM // 2
ROPE_BASE = 10000.0
GRID_W = 64
RG_BLOCKS = 8
RG_BD = D_RNN // RG_BLOCKS
RG_C = 8.0
CONV_W = 4
CONV_LEFT = 2
N_GROUPS = 4
EXP_PER_GROUP = 8
N_EXPERTS = N_GROUPS * EXP_PER_GROUP
TOP_K = 2
D_EXPERT = 512
D_IN = 2 * D_RNN + D_ATTN + 2 * D_KV
LN_EPS = 1e-6
NEG_INF = -1e30

LANES = 128
SUBLANES = 8
LOG2_E = 1.4426950408889634
Q_SCALE = ATTN_SCALE * LOG2_E
ROW_TILE = 256
ROW_GROUP = 3
ROUTE_TILES = (1024, 512, 256)
SCAN_TILE = 128
MOE_TILE = 512
LOGIT_W = LANES
VMEM_LIMIT = 56 * 1024 * 1024

_F32 = jnp.float32
_BF16 = jnp.bfloat16


def _cparams(*sem):
    return pltpu.CompilerParams(dimension_semantics=sem, vmem_limit_bytes=VMEM_LIMIT)


def _sigmoid(x):
    return 1.0 / (1.0 + jnp.exp(-x))


def _dot(a, b):
    return jnp.dot(a, b, preferred_element_type=_F32)


def _mod_kernel(c_ref, w_ref, b_ref, o_ref):
    c = c_ref[...]
    a = c * _sigmoid(c)
    a_hi = a.astype(_BF16)
    a_lo = (a - a_hi.astype(_F32)).astype(_BF16)
    w = w_ref[0]
    w_hi = w.astype(_BF16)
    w_lo = (w - w_hi.astype(_F32)).astype(_BF16)
    o_ref[0] = _dot(a_hi, w_hi) + (_dot(a_lo, w_hi) + _dot(a_hi, w_lo)) + b_ref[0]


def _modulation(c_all, w_mod, b_mod):
    depth, d, n = w_mod.shape
    rows = c_all.shape[0]
    tn = 1536
    return pl.pallas_call(
        _mod_kernel,
        grid=(depth, n // tn),
        in_specs=[
            pl.BlockSpec((rows, d), lambda l, j: (0, 0)),
            pl.BlockSpec((1, d, tn), lambda l, j: (l, 0, j)),
            pl.BlockSpec((1, 1, tn), lambda l, j: (l, 0, j)),
        ],
        out_specs=pl.BlockSpec((1, rows, tn), lambda l, j: (l, 0, j)),
        out_shape=jax.ShapeDtypeStruct((depth, rows, n), _F32),
        compiler_params=_cparams("arbitrary", "arbitrary"),
        name="modulation",
    )(c_all, w_mod, b_mod.reshape(depth, 1, n))


def _normalize(x):
    mu = jnp.mean(x, axis=-1, keepdims=True)
    xc = x - mu
    var = jnp.mean(xc * xc, axis=-1, keepdims=True)
    return xc * lax.rsqrt(var + LN_EPS)


def _rope128(x, cos, sin_a, sin_b):
    return (x * cos + pltpu.roll(x, LANES - AXIS_DIM // 2, 1) * sin_a
            + pltpu.roll(x, AXIS_DIM // 2, 1) * sin_b)


ROW_TILE_SUB = D_MODEL // LANES


def _store_row_tiles(ref, base, x):
    rows = x.shape[0]
    for s in range(ROW_TILE_SUB):
        ref[pl.ds(base * ROW_TILE_SUB + s, rows, stride=ROW_TILE_SUB), :] = x[:, s * LANES:(s + 1) * LANES]


def _load_row_tiles(ref, base, rows):
    return jnp.concatenate(
        [ref[pl.ds(base * ROW_TILE_SUB + s, rows, stride=ROW_TILE_SUB), :] for s in range(ROW_TILE_SUB)],
        axis=-1)


def _mod_index(b, j, n_lat_tiles, n_batch):
    return jnp.where(j >= n_lat_tiles, n_batch, b)


def _sub_tile_mod(s, n_sub, n_batch):
    b = pl.program_id(0)
    if s != n_sub - 1:
        return b
    return jnp.where(pl.program_id(1) == pl.num_programs(1) - 1, n_batch, b)


def _inproj_kernel(n_batch, x_ref, mods_ref, w_ref, cos_ref, sa_ref, sb_ref,
                   u_ref, g_ref, q_ref, kt_ref, v_ref):
    for s in range(x_ref.shape[1] // ROW_TILE):
        rows = slice(s * ROW_TILE, (s + 1) * ROW_TILE)
        m = _sub_tile_mod(s, x_ref.shape[1] // ROW_TILE, n_batch)
        h = _normalize(x_ref[0, rows]) * (1.0 + mods_ref[m, 1:2, :]) + mods_ref[m, 0:1, :]
        y = _dot(h.astype(_BF16), w_ref[...])
        u_ref[0, rows] = y[:, :D_RNN]
        g_ref[0, rows] = y[:, D_RNN:2 * D_RNN]
        cos, sa, sb = cos_ref[rows], sa_ref[rows], sb_ref[rows]
        q0 = 2 * D_RNN
        for c in range(D_ATTN // LANES):
            qc = _rope128(y[:, q0 + c * LANES:q0 + (c + 1) * LANES], cos, sa, sb)
            q_ref[0, rows, c * LANES:(c + 1) * LANES] = (qc * Q_SCALE).astype(_BF16)
        k0 = q0 + D_ATTN
        kt_ref[0, :, rows] = _rope128(y[:, k0:k0 + D_KV], cos, sa, sb).T.astype(_BF16)
        v_ref[0, rows] = y[:, k0 + D_KV:k0 + 2 * D_KV].astype(_BF16)


def _inproj(x, mods, w_in, rope):
    nb, r, d = x.shape
    tm = ROW_TILE * ROW_GROUP
    nj = r // tm
    cos, sa, sb = rope
    rope_spec = pl.BlockSpec((tm, LANES), lambda b, j: (j, 0))
    row = lambda w: pl.BlockSpec((1, tm, w), lambda b, j: (b, j, 0))
    return pl.pallas_call(
        functools.partial(_inproj_kernel, nb),
        grid=(nb, nj),
        in_specs=[
            row(d),
            pl.BlockSpec(mods.shape, lambda b, j: (0, 0, 0)),
            pl.BlockSpec((d, D_IN), lambda b, j: (0, 0)),
            rope_spec, rope_spec, rope_spec,
        ],
        out_specs=[row(D_RNN), row(D_RNN), row(D_ATTN),
                   pl.BlockSpec((1, D_KV, tm), lambda b, j: (b, 0, j)), row(D_KV)],
        out_shape=[
            jax.ShapeDtypeStruct((nb, r, D_RNN), _F32),
            jax.ShapeDtypeStruct((nb, r, D_RNN), _F32),
            jax.ShapeDtypeStruct((nb, r, D_ATTN), _BF16),
            jax.ShapeDtypeStruct((nb, D_KV, r), _BF16),
            jax.ShapeDtypeStruct((nb, r, D_KV), _BF16),
        ],
        compiler_params=_cparams("arbitrary", "arbitrary"),
        name="inproj",
    )(x, mods, w_in, cos, sa, sb)


def _scan_block(step, reverse, n_lat, n_ctx):
    in_ctx = step < n_ctx
    i_ctx = (n_ctx - 1 - step) if reverse else step
    s_lat = step - n_ctx
    i_lat = (n_lat - 1 - s_lat) if reverse else s_lat
    blk = jnp.where(in_ctx, n_lat + i_ctx, i_lat)
    first = jnp.where(in_ctx, i_ctx == 0, i_lat == 0)
    last = jnp.where(in_ctx, i_ctx == n_ctx - 1, i_lat == n_lat - 1)
    return blk, first, last


def _scan_direction(x, w_ref, bias_ref, lam_ref, a_s, b_s, h_s, carry, reverse):
    nb, ts, ch = x.shape
    stride = a_s.shape[1] // nb
    x2 = x.reshape(nb * ts, ch)
    z = _dot(x2.astype(_BF16), w_ref[...]) + bias_ref[...]
    gates = 0.5 * jnp.tanh(0.5 * z) + 0.5
    r = gates[:, :ch]
    gi = gates[:, ch:]
    neg_lam = -lam_ref[...]
    softplus = jnp.maximum(neg_lam, 0.0) + jnp.log1p(jnp.exp(-jnp.abs(neg_lam)))
    log_a = (-RG_C) * r * softplus
    a = jnp.exp(log_a)
    t2 = jnp.tanh(-log_a) * (1.0 + a * a)
    b = jnp.where(t2 > 0.0, t2 * lax.rsqrt(t2), 0.0) * (gi * x2)
    n_chunk = ch // LANES
    for c in range(n_chunk):
        for i in range(nb):
            a_s[c, i * stride:i * stride + ts, :] = a[i * ts:(i + 1) * ts, c * LANES:(c + 1) * LANES]
            b_s[c, i * stride:i * stride + ts, :] = b[i * ts:(i + 1) * ts, c * LANES:(c + 1) * LANES]

    def body(i, hs):
        t = (ts - 1 - i) if reverse else i
        rows = pl.ds(t, nb, stride=stride)
        new = []
        for c in range(n_chunk):
            hc = a_s[c, rows, :] * hs[c] + b_s[c, rows, :]
            h_s[c, rows, :] = hc
            new.append(hc)
        return tuple(new)

    h0 = tuple(carry[c] for c in range(n_chunk))
    h_end = lax.fori_loop(0, ts, body, h0, unroll=8)
    for c in range(n_chunk):
        carry[c] = h_end[c]
    return jnp.stack(
        [jnp.concatenate([h_s[c, i * stride:i * stride + ts, :] for c in range(n_chunk)], axis=-1)
         for i in range(nb)], axis=0)


def _rglru_fwd_kernel(n_lat, n_ctx, up_ref, uc_ref, un_ref, cw_ref, cb_ref, w_ref, bias_ref, lam_ref,
                      x_ref, h_ref, ubuf, a_s, b_s, h_s, carry):
    nb, ts, ch = uc_ref.shape
    step = pl.program_id(0)
    _, first, last = _scan_block(step, False, n_lat, n_ctx)

    @pl.when(step == 0)
    def _():
        carry[...] = jnp.zeros_like(carry)

    ubuf[:, SUBLANES:SUBLANES + ts, :] = uc_ref[...]
    ubuf[:, 0:SUBLANES, :] = jnp.where(first, 0.0, up_ref[...])
    ubuf[:, SUBLANES + ts:, :] = jnp.where(last, 0.0, un_ref[...])
    x = cb_ref[...][None]
    for j in range(CONV_W):
        off = SUBLANES + j - CONV_LEFT
        x = x + ubuf[:, off:off + ts, :] * cw_ref[j:j + 1, :][None]
    x_ref[...] = x
    h_ref[...] = _scan_direction(x, w_ref, bias_ref, lam_ref, a_s, b_s, h_s, carry, False)


def _rglru_bwd_kernel(x_ref, w_ref, bias_ref, lam_ref, hf_ref, g_ref, gn_ref, o_ref, a_s, b_s, h_s, carry):
    @pl.when(pl.program_id(0) == 0)
    def _():
        carry[...] = jnp.zeros_like(carry)

    h = _scan_direction(x_ref[...], w_ref, bias_ref, lam_ref, a_s, b_s, h_s, carry, True)
    g = g_ref[...]
    gelu = 0.5 * g * (1.0 + jnp.tanh(0.7978845608028654 * (g + 0.044715 * (g * g * g))))
    y = (hf_ref[...] + h) * gelu
    ms = jnp.mean(y * y, axis=-1, keepdims=True)
    o_ref[...] = (y * lax.rsqrt(ms + LN_EPS) * gn_ref[...][None]).astype(o_ref.dtype)


def _scan_scratch(nb, ts, ch):
    rows = nb * (ts + SUBLANES)
    return [pltpu.VMEM((ch // LANES, rows, LANES), _F32)] * 3 + [pltpu.VMEM((ch // LANES, nb, LANES), _F32)]


def _rglru_fwd(u, conv_w, conv_b, w_gate, b_gate, lam, n_lat, n_ctx):
    nb, r, ch = u.shape
    ts = SCAN_TILE
    halo = ts // SUBLANES
    n_halo = r // SUBLANES
    blk_of = lambda s: _scan_block(s, False, n_lat, n_ctx)[0]
    full = lambda shape: pl.BlockSpec(shape, lambda s: (0,) * len(shape))
    blk = pl.BlockSpec((nb, ts, ch), lambda s: (0, blk_of(s), 0))
    return pl.pallas_call(
        functools.partial(_rglru_fwd_kernel, n_lat, n_ctx),
        grid=(n_lat + n_ctx,),
        in_specs=[
            pl.BlockSpec((nb, SUBLANES, ch), lambda s: (0, jnp.maximum(blk_of(s) * halo - 1, 0), 0)),
            blk,
            pl.BlockSpec((nb, SUBLANES, ch), lambda s: (0, jnp.minimum((blk_of(s) + 1) * halo, n_halo - 1), 0)),
            full((CONV_W, ch)), full((1, ch)), full((ch, 2 * ch)), full((1, 2 * ch)), full((1, ch)),
        ],
        out_specs=[blk, blk],
        out_shape=[jax.ShapeDtypeStruct((nb, r, ch), _F32)] * 2,
        scratch_shapes=[pltpu.VMEM((nb, ts + 2 * SUBLANES, ch), _F32)] + _scan_scratch(nb, ts, ch),
        compiler_params=_cparams("arbitrary"),
        name="rglru_fwd",
    )(u, u, u, conv_w, conv_b, w_gate, b_gate, lam)


def _rglru_bwd(x, w_gate, b_gate, lam, hf, g, gn, n_lat, n_ctx):
    nb, r, ch = x.shape
    ts = SCAN_TILE
    full = lambda shape: pl.BlockSpec(shape, lambda s: (0,) * len(shape))
    blk = pl.BlockSpec((nb, ts, ch), lambda s: (0, _scan_block(s, True, n_lat, n_ctx)[0], 0))
    return pl.pallas_call(
        _rglru_bwd_kernel,
        grid=(n_lat + n_ctx,),
        in_specs=[blk, full((ch, 2 * ch)), full((1, 2 * ch)), full((1, ch)), blk, blk, full((1, ch))],
        out_specs=blk,
        out_shape=jax.ShapeDtypeStruct((nb, r, ch), _BF16),
        scratch_shapes=_scan_scratch(nb, ts, ch),
        compiler_params=_cparams("arbitrary"),
        name="rglru_bwd",
    )(x, w_gate, b_gate, lam, hf, g, gn)


def _attn_kernel(n_lat, sink_ref, q_ref, kp_ref, km_ref, kn_ref, kx_ref,
                 vp_ref, vm_ref, vn_ref, vx_ref, gn_ref, o_ref):
    n0 = 2 * pl.program_id(1)
    kt_mid, v_mid = km_ref[0], vm_ref[0]
    kt_lo, kt_hi = kt_mid[:, :BLK], kt_mid[:, BLK:]
    v_lo, v_hi = v_mid[:BLK], v_mid[BLK:]
    kt_x, v_x = kx_ref[0], vx_ref[0]
    for half, (kts, vs) in enumerate((((kp_ref[0], kt_lo, kt_hi), (vp_ref[0], v_lo, v_hi)),
                                      ((kt_lo, kt_hi, kn_ref[0]), (v_lo, v_hi, vn_ref[0])))):
        rows = slice(half * BLK, (half + 1) * BLK)
        kt_all = jnp.concatenate(list(kts) + [kt_x], axis=1)
        v_all = jnp.concatenate(list(vs) + [v_x], axis=0)
        out = _attend_block(n_lat, n0 + half, sink_ref, q_ref[0, rows], kt_all, v_all, gn_ref)
        for c, ch in enumerate(out):
            o_ref[0, rows, c * LANES:(c + 1) * LANES] = ch.astype(o_ref.dtype)


def _attend_block(n_lat, n, sink_ref, q, kt_all, v_all, gn_ref):
    n_keys = kt_all.shape[1]
    is_lat = n < n_lat
    lo = jnp.where(is_lat, jnp.where(n >= 1, 0, BLK), 3 * BLK)
    hi = jnp.where(is_lat, jnp.where(n + 1 < n_lat, 3 * BLK, 2 * BLK), 0)
    col = lax.broadcasted_iota(jnp.int32, (BLK, n_keys), 1)
    row = lax.broadcasted_iota(jnp.int32, (BLK, n_keys), 0)
    in_win = jnp.abs(col - BLK - row) <= WINDOW
    mask = ((col >= lo) & (col < hi) & in_win) | (col >= 3 * BLK)

    lane1 = lax.broadcasted_iota(jnp.int32, (1, LANES), 1)
    keep_lo = jnp.where(lane1 < HEAD_DIM, 1.0, 0.0).astype(_BF16)
    keep_hi = jnp.where(lane1 >= HEAD_DIM, 1.0, 0.0).astype(_BF16)
    lane_o = lax.broadcasted_iota(jnp.int32, (BLK, LANES), 1)
    chunks = []
    for hk in range(N_KV_HEADS):
        kt_h = kt_all[hk * HEAD_DIM:(hk + 1) * HEAD_DIM, :]
        kt2 = jnp.concatenate([kt_h, kt_h], axis=0)
        qs = []
        for cc in range(GQ // 2):
            c = hk * (GQ // 2) + cc
            qc = q[:, c * LANES:(c + 1) * LANES]
            qs += [qc * keep_lo, qc * keep_hi]
        s_all = _dot(jnp.concatenate(qs, axis=0), kt2)
        ps, invs = [], []
        for g in range(GQ):
            s = jnp.where(mask, s_all[g * BLK:(g + 1) * BLK], NEG_INF)
            sink = sink_ref[hk * GQ + g] * LOG2_E
            m = jnp.maximum(jnp.max(s, axis=-1, keepdims=True), sink)
            p = jnp.exp2(s - m)
            invs.append(1.0 / (jnp.sum(p, axis=-1, keepdims=True) + jnp.exp2(sink - m)))
            ps.append(p.astype(_BF16))
        o_all = _dot(jnp.concatenate(ps, axis=0), v_all)
        for cc in range(GQ // 2):
            even = o_all[(2 * cc) * BLK:(2 * cc + 1) * BLK] * invs[2 * cc]
            odd = o_all[(2 * cc + 1) * BLK:(2 * cc + 2) * BLK] * invs[2 * cc + 1]
            if hk == 0:
                odd = pltpu.roll(odd, HEAD_DIM, 1)
            else:
                even = pltpu.roll(even, HEAD_DIM, 1)
            chunks.append(jnp.where(lane_o < HEAD_DIM, even, odd))
    ssq = sum(jnp.sum(ch * ch, axis=-1, keepdims=True) for ch in chunks)
    scale = lax.rsqrt(ssq * (1.0 / D_ATTN) + LN_EPS)
    return [ch * scale * gn_ref[:, c * LANES:(c + 1) * LANES] for c, ch in enumerate(chunks)]


def _attention(q, kt, v, sink, gn, n_lat, ctx_len):
    nb, r, _ = q.shape
    pair = 2 * BLK
    assert (n_lat * BLK) % pair == 0 and ctx_len % pair == 0
    ctx_blk = (n_lat * BLK) // ctx_len
    n_pairs = n_lat // 2
    i_prev = lambda i: jnp.clip(2 * i - 1, 0, n_lat - 1)
    i_next = lambda i: jnp.clip(2 * i + 2, 0, n_lat - 1)
    i_mid = lambda i: jnp.minimum(i, n_pairs - 1)
    edge = lambda f: pl.BlockSpec((1, BLK, D_KV), lambda b, i: (b, f(i), 0))
    edge_t = lambda f: pl.BlockSpec((1, D_KV, BLK), lambda b, i: (b, 0, f(i)))
    mid = pl.BlockSpec((1, pair, D_KV), lambda b, i: (b, i_mid(i), 0))
    mid_t = pl.BlockSpec((1, D_KV, pair), lambda b, i: (b, 0, i_mid(i)))
    ctx = pl.BlockSpec((1, ctx_len, D_KV), lambda b, i: (b, ctx_blk, 0))
    ctx_t = pl.BlockSpec((1, D_KV, ctx_len), lambda b, i: (b, 0, ctx_blk))
    return pl.pallas_call(
        functools.partial(_attn_kernel, n_lat),
        grid=(nb, r // pair),
        in_specs=[
            pl.BlockSpec(memory_space=pltpu.SMEM),
            pl.BlockSpec((1, pair, D_ATTN), lambda b, i: (b, i, 0)),
            edge_t(i_prev), mid_t, edge_t(i_next), ctx_t, edge(i_prev), mid, edge(i_next), ctx,
            pl.BlockSpec((1, D_ATTN), lambda b, i: (0, 0)),
        ],
        out_specs=pl.BlockSpec((1, pair, D_ATTN), lambda b, i: (b, i, 0)),
        out_shape=jax.ShapeDtypeStruct((nb, r, D_ATTN), _BF16),
        compiler_params=_cparams("arbitrary", "arbitrary"),
        name="attention",
    )(sink, q, kt, kt, kt, kt, v, v, v, v, gn)


def _first_lane(cond, lane):
    return jnp.min(jnp.where(cond, lane, LOGIT_W), axis=-1, keepdims=True)


def _route_tile(logits, counts, lower):
    lane = lax.broadcasted_iota(jnp.int32, logits.shape, 1)
    is_group = lane < N_GROUPS
    g_max = jnp.max(jnp.where(is_group, logits, -jnp.inf), axis=-1, keepdims=True)
    g_sel = _first_lane(is_group & (logits == g_max), lane)
    p_g = 1.0 / jnp.sum(jnp.where(is_group, jnp.exp(logits - g_max), 0.0), axis=-1, keepdims=True)
    e_lo = N_GROUPS + EXP_PER_GROUP * g_sel
    in_grp = (lane >= e_lo) & (lane < e_lo + EXP_PER_GROUP)
    v1 = jnp.max(jnp.where(in_grp, logits, -jnp.inf), axis=-1, keepdims=True)
    i1 = _first_lane(in_grp & (logits == v1), lane)
    rest = in_grp & (lane != i1)
    v2 = jnp.max(jnp.where(rest, logits, -jnp.inf), axis=-1, keepdims=True)
    i2 = _first_lane(rest & (logits == v2), lane)
    e21 = jnp.exp(v2 - v1)
    gate1 = p_g / (1.0 + e21)
    gate2 = gate1 * e21
    eid1 = i1 - N_GROUPS
    eid2 = i2 - N_GROUPS
    oh1 = lane == eid1
    oh2 = lane == eid2
    onehot = jnp.where(oh1 | oh2, 1.0, 0.0)
    before = _dot(lower, onehot.astype(_BF16)) + counts
    rank1 = jnp.sum(jnp.where(oh1, before, 0.0), axis=-1, keepdims=True)
    rank2 = jnp.sum(jnp.where(oh2, before, 0.0), axis=-1, keepdims=True)
    new_counts = counts + jnp.sum(onehot, axis=0, keepdims=True)
    fields = (eid1.astype(_F32), eid2.astype(_F32), rank1, rank2, gate1, gate2)
    slab = jnp.zeros(logits.shape, _F32)
    for i, f in enumerate(fields):
        slab = jnp.where(lane == i, f, slab)
    return slab, new_counts


def _outproj_kernel(alpha, n_batch, rn_ref, an_ref, x_ref, mods_ref, wr_ref, wa_ref, lg_ref, lb_ref,
                    wrt_ref, brt_ref, x1_ref, h2_ref, lo_ref):
    for s in range(x_ref.shape[1] // ROW_TILE):
        rows = slice(s * ROW_TILE, (s + 1) * ROW_TILE)
        m = _sub_tile_mod(s, x_ref.shape[1] // ROW_TILE, n_batch)
        y = _dot(rn_ref[0, rows], wr_ref[...]) + _dot(an_ref[0, rows], wa_ref[...])
        z = alpha * x_ref[0, rows] + mods_ref[m, 2:3, :] * y
        x1 = _normalize(z) * lg_ref[...] + lb_ref[...]
        x1_ref[0, rows] = x1
        h2 = _normalize(x1) * (1.0 + mods_ref[m, 4:5, :]) + mods_ref[m, 3:4, :]
        _store_row_tiles(h2_ref, s * ROW_TILE, h2)
        lo_ref[0, rows] = _dot(h2.astype(_BF16), wrt_ref[...]) + brt_ref[...]


def _outproj(rn, an, x, mods, w_out, ln_g, ln_b, w_rt, b_rt, alpha):
    nb, r, d = x.shape
    tm = ROW_TILE * ROW_GROUP
    row = lambda w: pl.BlockSpec((1, tm, w), lambda b, j: (b, j, 0))
    full = lambda shape: pl.BlockSpec(shape, lambda b, j: (0,) * len(shape))
    return pl.pallas_call(
        functools.partial(_outproj_kernel, alpha, nb),
        grid=(nb, r // tm),
        in_specs=[
            row(D_RNN), row(D_ATTN), row(d), full(mods.shape),
            pl.BlockSpec((D_RNN, d), lambda b, j: (0, 0)),
            pl.BlockSpec((D_ATTN, d), lambda b, j: (1, 0)),
            full((1, d)), full((1, d)), full((d, LOGIT_W)), full((1, LOGIT_W)),
        ],
        out_specs=[row(d), pl.BlockSpec((tm * ROW_TILE_SUB, LANES), lambda b, j: (b * (r // tm) + j, 0)),
                   row(LOGIT_W)],
        out_shape=[
            jax.ShapeDtypeStruct((nb, r, d), _F32),
            jax.ShapeDtypeStruct((nb * r * ROW_TILE_SUB, LANES), _F32),
            jax.ShapeDtypeStruct((nb, r, LOGIT_W), _F32),
        ],
        compiler_params=_cparams("arbitrary", "arbitrary"),
        name="outproj",
    )(rn, an, x, mods, w_out, w_out, ln_g, ln_b, w_rt, b_rt)


def _route_kernel(lg_ref, low_ref, rt_ref, cnt_ref, cnt_s):
    @pl.when(pl.program_id(0) == 0)
    def _():
        cnt_s[...] = jnp.zeros_like(cnt_s)

    slab, counts = _route_tile(lg_ref[...], cnt_s[...], low_ref[...])
    rt_ref[...] = slab
    cnt_s[...] = counts
    cnt_ref[...] = counts


def _route_slab(logits):
    nt = logits.shape[0]
    tm = next(t for t in ROUTE_TILES if nt % t == 0)
    lower = jnp.tril(jnp.ones((tm, tm), _BF16), -1)
    return pl.pallas_call(
        _route_kernel,
        grid=(nt // tm,),
        in_specs=[pl.BlockSpec((tm, LOGIT_W), lambda i: (i, 0)), pl.BlockSpec((tm, tm), lambda i: (0, 0))],
        out_specs=[pl.BlockSpec((tm, LOGIT_W), lambda i: (i, 0)), pl.BlockSpec((1, LOGIT_W), lambda i: (0, 0))],
        out_shape=[jax.ShapeDtypeStruct((nt, LOGIT_W), _F32), jax.ShapeDtypeStruct((1, LOGIT_W), _F32)],
        scratch_shapes=[pltpu.VMEM((1, LOGIT_W), _F32)],
        compiler_params=_cparams("arbitrary"),
        name="route",
    )(logits, lower)


def _tile_rows(i):
    return pl.ds(pl.multiple_of(i * ROW_TILE_SUB, ROW_TILE_SUB), ROW_TILE_SUB)


def _row_gather(idx_ref, n_rows, src_hbm, dst, dst_base, sem):
    def body(i, carry):
        t = idx_ref[0, 0, i]
        pltpu.make_async_copy(src_hbm.at[_tile_rows(t), :], dst.at[_tile_rows(dst_base + i), :], sem).start()
        return carry

    lax.fori_loop(0, n_rows, body, 0, unroll=8)


def _gather_wait(n_rows, src_hbm, dst, dst_base, sem):
    pltpu.make_async_copy(src_hbm.at[pl.ds(0, n_rows * ROW_TILE_SUB), :],
                          dst.at[pl.ds(pl.multiple_of(dst_base * ROW_TILE_SUB, ROW_TILE_SUB),
                                       n_rows * ROW_TILE_SUB), :], sem).wait()


def _moe_kernel(blk_e_ref, tok_ref, tok_next_ref, h_hbm, w1_ref, w3_ref, w2_ref, o_ref,
                xbuf, sem, w1_s, w3_s, w2_s):
    j = pl.program_id(0)
    n_used = blk_e_ref[pl.num_programs(0)]
    slot = j % 2
    bm = tok_ref.shape[2]

    @pl.when(j == 0)
    def _():
        _row_gather(tok_ref, bm, h_hbm, xbuf, 0, sem.at[0])

    @pl.when(j + 1 < n_used)
    def _():
        _row_gather(tok_next_ref, bm, h_hbm, xbuf, (1 - slot) * bm, sem.at[1 - slot])

    @pl.when(j < n_used)
    def _():
        @pl.when((j == 0) | (blk_e_ref[j] != blk_e_ref[jnp.maximum(j - 1, 0)]))
        def _():
            w1_s[...] = w1_ref[0, 0].astype(_BF16)
            w3_s[...] = w3_ref[0, 0].astype(_BF16)
            w2_s[...] = w2_ref[0, 0].astype(_BF16)

        _gather_wait(bm, h_hbm, xbuf, slot * bm, sem.at[slot])
        x = _load_row_tiles(xbuf, slot * bm, bm).astype(_BF16)
        a = _dot(x, w1_s[...])
        h = (a * _sigmoid(a)) * _dot(x, w3_s[...])
        _store_row_tiles(o_ref, 0, _dot(h.astype(_BF16), w2_s[...]))

    @pl.when(j >= n_used)
    def _():
        o_ref[...] = jnp.zeros_like(o_ref)


def _moe_ffn(h2, slot_tok, blk_e, w1, w3, w2, layer):
    d = D_MODEL
    bm = MOE_TILE
    n_blk = blk_e.shape[0] - 1
    tok3 = slot_tok.reshape(n_blk, 1, bm)
    smem_blk = lambda f: pl.BlockSpec((1, 1, bm), lambda j, be: (f(j), 0, 0), memory_space=pltpu.SMEM)
    grid_spec = pltpu.PrefetchScalarGridSpec(
        num_scalar_prefetch=1,
        grid=(n_blk,),
        in_specs=[
            smem_blk(lambda j: j),
            smem_blk(lambda j: jnp.minimum(j + 1, n_blk - 1)),
            pl.BlockSpec(memory_space=pl.ANY),
            pl.BlockSpec((1, 1, d, D_EXPERT), lambda j, be: (layer, be[j], 0, 0)),
            pl.BlockSpec((1, 1, d, D_EXPERT), lambda j, be: (layer, be[j], 0, 0)),
            pl.BlockSpec((1, 1, D_EXPERT, d), lambda j, be: (layer, be[j], 0, 0)),
        ],
        out_specs=pl.BlockSpec((bm * ROW_TILE_SUB, LANES), lambda j, be: (j, 0)),
        scratch_shapes=[
            pltpu.VMEM((2 * bm * ROW_TILE_SUB, LANES), _F32), pltpu.SemaphoreType.DMA((2,)),
            pltpu.VMEM((d, D_EXPERT), _BF16), pltpu.VMEM((d, D_EXPERT), _BF16),
            pltpu.VMEM((D_EXPERT, d), _BF16),
        ],
    )
    return pl.pallas_call(
        _moe_kernel,
        grid_spec=grid_spec,
        out_shape=jax.ShapeDtypeStruct((n_blk * bm * ROW_TILE_SUB, LANES), _F32),
        compiler_params=_cparams("arbitrary"),
        name="moe_ffn",
    )(blk_e, tok3, tok3, h2, w1, w3, w2)


def _combine_kernel(alpha, dst_ref, dst_next_ref, y_hbm, gate_ref, x_ref, mod_ref, lg_ref, lb_ref,
                    o_ref, ybuf, sem):
    s = pl.program_id(0)
    n = pl.num_programs(0)
    slot = s % 2
    tm = x_ref.shape[0]

    nr = TOP_K * tm

    @pl.when(s == 0)
    def _():
        _row_gather(dst_ref, nr, y_hbm, ybuf, 0, sem.at[0])

    @pl.when(s + 1 < n)
    def _():
        _row_gather(dst_next_ref, nr, y_hbm, ybuf, (1 - slot) * nr, sem.at[1 - slot])

    _gather_wait(nr, y_hbm, ybuf, slot * nr, sem.at[slot])
    gate = gate_ref[...]
    y = (_load_row_tiles(ybuf, slot * nr, tm) * gate[:, 0:1]
         + _load_row_tiles(ybuf, slot * nr + tm, tm) * gate[:, 1:2])
    z = alpha * x_ref[...] + mod_ref[0, 5:6, :] * y
    o_ref[...] = _normalize(z) * lg_ref[...] + lb_ref[...]


def _combine(yb, dest, gate, x1, mods, ln_g, ln_b, n_batch, tiles_per_batch, tiles_used, alpha):
    nt, d = x1.shape
    tm = ROW_TILE
    n_steps = n_batch * tiles_used
    dst3 = dest.reshape(nt // tm, tm, TOP_K).transpose(0, 2, 1).reshape(nt // tm, 1, TOP_K * tm)

    def tile(s):
        return (s // tiles_used) * tiles_per_batch + s % tiles_used

    def mod_idx(s):
        return (jnp.where(s % tiles_used >= tiles_per_batch - 1, n_batch, s // tiles_used), 0, 0)

    smem_blk = lambda f: pl.BlockSpec((1, 1, TOP_K * tm), lambda s: (tile(f(s)), 0, 0),
                                      memory_space=pltpu.SMEM)
    full = lambda shape: pl.BlockSpec(shape, lambda s: (0,) * len(shape))
    return pl.pallas_call(
        functools.partial(_combine_kernel, alpha),
        grid=(n_steps,),
        in_specs=[
            smem_blk(lambda s: s),
            smem_blk(lambda s: jnp.minimum(s + 1, n_steps - 1)),
            pl.BlockSpec(memory_space=pl.ANY),
            pl.BlockSpec((tm, TOP_K), lambda s: (tile(s), 0)),
            pl.BlockSpec((tm, d), lambda s: (tile(s), 0)),
            pl.BlockSpec((1, 6, d), mod_idx),
            full((1, d)), full((1, d)),
        ],
        out_specs=pl.BlockSpec((tm, d), lambda s: (s, 0)),
        out_shape=jax.ShapeDtypeStruct((n_steps * tm, d), _F32),
        scratch_shapes=[pltpu.VMEM((2 * TOP_K * tm * ROW_TILE_SUB, LANES), _F32),
                        pltpu.SemaphoreType.DMA((2,))],
        compiler_params=_cparams("arbitrary"),
        name="combine_ln2",
    )(dst3, dst3, yb, gate, x1, mods, ln_g, ln_b)


def _route(slab, counts, n_blk):
    nt = slab.shape[0]
    bm = MOE_TILE
    eid = slab[:, 0:TOP_K].astype(jnp.int32)
    rank = slab[:, TOP_K:2 * TOP_K].astype(jnp.int32)
    gate = slab[:, 2 * TOP_K:3 * TOP_K]
    counts = counts[0, :N_EXPERTS].astype(jnp.int32)
    experts = jnp.arange(N_EXPERTS, dtype=jnp.int32)
    pcounts = (counts + bm - 1) // bm * bm
    pends = jnp.cumsum(pcounts)
    pstarts = pends - pcounts
    dest = jnp.sum(jnp.where(eid[:, :, None] == experts, pstarts, 0), axis=-1) + rank
    n_assign = nt * TOP_K
    n_fill = n_blk * bm - n_assign
    pad_ends = jnp.cumsum(pcounts - counts)
    fill_key = jnp.sum(pad_ends[None, :] <= jnp.arange(n_fill, dtype=jnp.int32)[:, None], axis=-1,
                       dtype=jnp.int32)
    keys = jnp.concatenate([eid.reshape(-1), fill_key])
    vals = jnp.concatenate([jnp.arange(n_assign, dtype=jnp.int32) // TOP_K, jnp.zeros((n_fill,), jnp.int32)])
    _, slot_tok = lax.sort_key_val(keys, vals)
    blk_start = jnp.arange(n_blk, dtype=jnp.int32) * bm
    blk_e = jnp.minimum(jnp.sum(pends[None, :] <= blk_start[:, None], axis=-1, dtype=jnp.int32), N_EXPERTS - 1)
    blk_meta = jnp.concatenate([blk_e, pends[-1:] // bm]).astype(jnp.int32)
    return dest, gate, slot_tok, blk_meta


def _rope_tables(seq, ident_rows):
    rows = seq // GRID_W
    row = jnp.repeat(jnp.arange(rows, dtype=_F32), GRID_W)
    col = jnp.tile(jnp.arange(GRID_W, dtype=_F32), rows)
    inv = ROPE_BASE ** (-jnp.arange(0, AXIS_DIM, 2, dtype=_F32) / AXIS_DIM)
    ang_r = row[:, None] * inv
    ang_c = col[:, None] * inv
    cr, sr, cc, sc = jnp.cos(ang_r), jnp.sin(ang_r), jnp.cos(ang_c), jnp.sin(ang_c)
    z = jnp.zeros_like(sr)
    reps = LANES // HEAD_DIM
    cos = jnp.tile(jnp.concatenate([cr, cr, cc, cc], -1), (1, reps))
    sin_a = jnp.tile(jnp.concatenate([-sr, z, -sc, z], -1), (1, reps))
    sin_b = jnp.tile(jnp.concatenate([z, sr, z, sc], -1), (1, reps))
    one = jnp.ones((ident_rows, LANES), _F32)
    zero = jnp.zeros((ident_rows, LANES), _F32)
    return (jnp.concatenate([cos, one], 0), jnp.concatenate([sin_a, zero], 0),
            jnp.concatenate([sin_b, zero], 0))


def _block_diag(w):
    eye = jnp.eye(RG_BLOCKS, dtype=w.dtype)
    return jnp.einsum("hij,hg->higj", w, eye).reshape(D_RNN, D_RNN)


def kernel(x, c, ctx, c_ctx, w_mod, b_mod, w_in, conv_w, conv_b, rg_wa, rg_ba, rg_wx, rg_bx, rg_lam,
           attn_sink, gn_rnn, gn_attn, w_out, ln1_g, ln1_b, router_g, router_gb, router_e, router_eb,
           w1, w3, w2, ln2_g, ln2_b):
    nb, seq, d = x.shape
    ctx_len = ctx.shape[1]
    depth = w_mod.shape[0]
    r = seq + ctx_len
    nt = nb * r
    alpha = (2 * depth) ** 0.25
    assert d == D_MODEL and seq % ROW_TILE == 0 and ctx_len % ROW_TILE == 0
    assert seq % GRID_W == 0 and seq % ctx_len == 0 and ctx_len % BLK == 0
    n_lat_tiles = seq // ROW_TILE
    tiles_per_batch = r // ROW_TILE
    assert tiles_per_batch == n_lat_tiles + 1, "one context tile per batch"
    assert tiles_per_batch % ROW_GROUP == 0
    n_blk = (nt * TOP_K + N_EXPERTS * (MOE_TILE - 1) + MOE_TILE - 1) // MOE_TILE

    pad_rows = (-(nb + 1)) % SUBLANES
    c_all = jnp.concatenate([c, c_ctx[None, :], jnp.zeros((pad_rows, d), _F32)], 0)
    mods_all = _modulation(c_all, w_mod, b_mod).reshape(depth, nb + 1 + pad_rows, 6, d)

    rope = _rope_tables(seq, ROW_TILE)
    xs = jnp.concatenate([x, ctx], axis=1)

    out = None
    for l in range(depth):
        last = l == depth - 1
        mods = mods_all[l]
        u, g, q, k, v = _inproj(xs, mods, w_in[l].astype(_BF16), rope)

        n_lat_s, n_ctx_s = seq // SCAN_TILE, ctx_len // SCAN_TILE
        gate_w = [jnp.concatenate([_block_diag(rg_wa[l, dr]), _block_diag(rg_wx[l, dr])], 1).astype(_BF16)
                  for dr in range(2)]
        gate_b = [jnp.concatenate([rg_ba[l, dr], rg_bx[l, dr]])[None, :] for dr in range(2)]
        xc, hf = _rglru_fwd(u, conv_w[l], conv_b[l][None, :], gate_w[0], gate_b[0], rg_lam[l, 0][None, :],
                            n_lat_s, n_ctx_s)
        rn = _rglru_bwd(xc, gate_w[1], gate_b[1], rg_lam[l, 1][None, :], hf, g, gn_rnn[l][None, :],
                        n_lat_s, n_ctx_s)

        an = _attention(q, k, v, attn_sink[l], gn_attn[l][None, :], seq // BLK, ctx_len)

        w_rt = jnp.concatenate([router_g[l], router_e[l],
                                jnp.zeros((d, LOGIT_W - N_GROUPS - N_EXPERTS), _F32)], 1).astype(_BF16)
        b_rt = jnp.concatenate([router_gb[l], router_eb[l],
                                jnp.zeros((LOGIT_W - N_GROUPS - N_EXPERTS,), _F32)])[None, :]
        x1, h2, logits = _outproj(rn, an, xs, mods, w_out[l].astype(_BF16), ln1_g[l][None, :],
                                  ln1_b[l][None, :], w_rt, b_rt, alpha)
        slab, counts = _route_slab(logits.reshape(nt, LOGIT_W))
        dest, gate, slot_tok, blk_e = _route(slab, counts, n_blk)
        yb = _moe_ffn(h2, slot_tok, blk_e, w1, w3, w2, l)
        tiles_used = n_lat_tiles if last else tiles_per_batch
        x2 = _combine(yb, dest, gate, x1.reshape(nt, d), mods, ln2_g[l][None, :], ln2_b[l][None, :],
                      nb, tiles_per_batch, tiles_used, alpha)
        if last:
            out = x2.reshape(nb, seq, d)
        else:
            xs = x2.reshape(nb, r, d)
    return out
```

```python
import functools

import jax
import jax.numpy as jnp
from jax import lax
from jax.experimental import pallas as pl
from jax.experimental.pallas import tpu as pltpu

D_MODEL = 1024
D_RNN = 512
D_ATTN = 512
HEAD_DIM = 64
N_Q_HEADS = 8
N_KV_HEADS = 2
GQ = N_Q_HEADS // N_KV_HEADS
D_KV = N_KV_HEADS * HEAD_DIM
WINDOW = 128
BLK = 128
ATTN_SCALE = HEAD_DIM ** -0.5
AXIS_DIM = HEAD_DIM // 2
ROPE_BASE = 10000.0
GRID_W = 64
RG_BLOCKS = 8
RG_BD = D_RNN // RG_BLOCKS
RG_C = 8.0
CONV_W = 4
CONV_LEFT = 2
N_GROUPS = 4
EXP_PER_GROUP = 8
N_EXPERTS = N_GROUPS * EXP_PER_GROUP
TOP_K = 2
D_EXPERT = 512
D_IN = 2 * D_RNN + D_ATTN + 2 * D_KV
LN_EPS = 1e-6
NEG_INF = -1e30

LANES = 128
SUBLANES = 8
LOG2_E = 1.4426950408889634
Q_SCALE = ATTN_SCALE * LOG2_E
ROW_TILE = 256
ROW_GROUP = 3
ROUTE_TILES = (1024, 512, 256)
SCAN_TILE = 128
MOE_TILE = 512
LOGIT_W = LANES
VMEM_LIMIT = 56 * 1024 * 1024

_F32 = jnp.float32
_BF16 = jnp.bfloat16


def _cparams(*sem):
    return pltpu.CompilerParams(dimension_semantics=sem, vmem_limit_bytes=VMEM_LIMIT)


def _sigmoid(x):
    return 1.0 / (1.0 + jnp.exp(-x))


def _dot(a, b):
    return jnp.dot(a, b, preferred_element_type=_F32)


def _mod_kernel(c_ref, w_ref, b_ref, o_ref):
    c = c_ref[...]
    a = c * _sigmoid(c)
    a_hi = a.astype(_BF16)
    a_lo = (a - a_hi.astype(_F32)).astype(_BF16)
    w = w_ref[0]
    w_hi = w.astype(_BF16)
    w_lo = (w - w_hi.astype(_F32)).astype(_BF16)
    o_ref[0] = _dot(a_hi, w_hi) + (_dot(a_lo, w_hi) + _dot(a_hi, w_lo)) + b_ref[0]


def _modulation(c_all, w_mod, b_mod):
    depth, d, n = w_mod.shape
    rows = c_all.shape[0]
    tn = 1536
    return pl.pallas_call(
        _mod_kernel,
        grid=(depth, n // tn),
        in_specs=[
            pl.BlockSpec((rows, d), lambda l, j: (0, 0)),
            pl.BlockSpec((1, d, tn), lambda l, j: (l, 0, j)),
            pl.BlockSpec((1, 1, tn), lambda l, j: (l, 0, j)),
        ],
        out_specs=pl.BlockSpec((1, rows, tn), lambda l, j: (l, 0, j)),
        out_shape=jax.ShapeDtypeStruct((depth, rows, n), _F32),
        compiler_params=_cparams("arbitrary", "arbitrary"),
        name="modulation",
    )(c_all, w_mod, b_mod.reshape(depth, 1, n))


def _normalize(x):
    mu = jnp.mean(x, axis=-1, keepdims=True)
    xc = x - mu
    var = jnp.mean(xc * xc, axis=-1, keepdims=True)
    return xc * lax.rsqrt(var + LN_EPS)


def _rope128(x, cos, sin_a, sin_b):
    return (x * cos + pltpu.roll(x, LANES - AXIS_DIM // 2, 1) * sin_a
            + pltpu.roll(x, AXIS_DIM // 2, 1) * sin_b)


ROW_TILE_SUB = D_MODEL // LANES


def _store_row_tiles(ref, base, x):
    rows = x.shape[0]
    for s in range(ROW_TILE_SUB):
        ref[pl.ds(base * ROW_TILE_SUB + s, rows, stride=ROW_TILE_SUB), :] = x[:, s * LANES:(s + 1) * LANES]


def _load_row_tiles(ref, base, rows):
    return jnp.concatenate(
        [ref[pl.ds(base * ROW_TILE_SUB + s, rows, stride=ROW_TILE_SUB), :] for s in range(ROW_TILE_SUB)],
        axis=-1)


def _mod_index(b, j, n_lat_tiles, n_batch):
    return jnp.where(j >= n_lat_tiles, n_batch, b)


def _sub_tile_mod(s, n_sub, n_batch):
    b = pl.program_id(0)
    if s != n_sub - 1:
        return b
    return jnp.where(pl.program_id(1) == pl.num_programs(1) - 1, n_batch, b)


def _inproj_kernel(n_batch, x_ref, mods_ref, w_ref, cos_ref, sa_ref, sb_ref,
                   u_ref, g_ref, q_ref, kt_ref, v_ref):
    for s in range(x_ref.shape[1] // ROW_TILE):
        rows = slice(s * ROW_TILE, (s + 1) * ROW_TILE)
        m = _sub_tile_mod(s, x_ref.shape[1] // ROW_TILE, n_batch)
        h = _normalize(x_ref[0, rows]) * (1.0 + mods_ref[m, 1:2, :]) + mods_ref[m, 0:1, :]
        y = _dot(h.astype(_BF16), w_ref[...])
        u_ref[0, rows] = y[:, :D_RNN]
        g_ref[0, rows] = y[:, D_RNN:2 * D_RNN]
        cos, sa, sb = cos_ref[rows], sa_ref[rows], sb_ref[rows]
        q0 = 2 * D_RNN
        for c in range(D_ATTN // LANES):
            qc = _rope128(y[:, q0 + c * LANES:q0 + (c + 1) * LANES], cos, sa, sb)
            q_ref[0, rows, c * LANES:(c + 1) * LANES] = (qc * Q_SCALE).astype(_BF16)
        k0 = q0 + D_ATTN
        kt_ref[0, :, rows] = _rope128(y[:, k0:k0 + D_KV], cos, sa, sb).T.astype(_BF16)
        v_ref[0, rows] = y[:, k0 + D_KV:k0 + 2 * D_KV].astype(_BF16)


def _inproj(x, mods, w_in, rope):
    nb, r, d = x.shape
    tm = ROW_TILE * ROW_GROUP
    nj = r // tm
    cos, sa, sb = rope
    rope_spec = pl.BlockSpec((tm, LANES), lambda b, j: (j, 0))
    row = lambda w: pl.BlockSpec((1, tm, w), lambda b, j: (b, j, 0))
    return pl.pallas_call(
        functools.partial(_inproj_kernel, nb),
        grid=(nb, nj),
        in_specs=[
            row(d),
            pl.BlockSpec(mods.shape, lambda b, j: (0, 0, 0)),
            pl.BlockSpec((d, D_IN), lambda b, j: (0, 0)),
            rope_spec, rope_spec, rope_spec,
        ],
        out_specs=[row(D_RNN), row(D_RNN), row(D_ATTN),
                   pl.BlockSpec((1, D_KV, tm), lambda b, j: (b, 0, j)), row(D_KV)],
        out_shape=[
            jax.ShapeDtypeStruct((nb, r, D_RNN), _F32),
            jax.ShapeDtypeStruct((nb, r, D_RNN), _F32),
            jax.ShapeDtypeStruct((nb, r, D_ATTN), _BF16),
            jax.ShapeDtypeStruct((nb, D_KV, r), _BF16),
            jax.ShapeDtypeStruct((nb, r, D_KV), _BF16),
        ],
        compiler_params=_cparams("arbitrary", "arbitrary"),
        name="inproj",
    )(x, mods, w_in, cos, sa, sb)


def _scan_block(step, reverse, n_lat, n_ctx):
    in_ctx = step < n_ctx
    i_ctx = (n_ctx - 1 - step) if reverse else step
    s_lat = step - n_ctx
    i_lat = (n_lat - 1 - s_lat) if reverse else s_lat
    blk = jnp.where(in_ctx, n_lat + i_ctx, i_lat)
    first = jnp.where(in_ctx, i_ctx == 0, i_lat == 0)
    last = jnp.where(in_ctx, i_ctx == n_ctx - 1, i_lat == n_lat - 1)
    return blk, first, last


def _scan_direction(x, w_ref, bias_ref, lam_ref, a_s, b_s, h_s, carry, reverse):
    nb, ts, ch = x.shape
    stride = a_s.shape[1] // nb
    x2 = x.reshape(nb * ts, ch)
    th = jnp.tanh(_dot(x2.astype(_BF16), w_ref[...]) + bias_ref[...])
    gi = 0.5 * th[:, ch:] + 0.5
    neg_lam = -lam_ref[...]
    softplus = jnp.maximum(neg_lam, 0.0) + jnp.log1p(jnp.exp(-jnp.abs(neg_lam)))
    c2 = (-0.5 * RG_C * LOG2_E) * softplus
    log2_a = c2 * th[:, :ch] + c2
    a = jnp.exp2(log2_a)
    t2 = jnp.tanh(log2_a * (-1.0 / LOG2_E)) * (1.0 + a * a)
    b = jnp.where(t2 > 0.0, t2 * lax.rsqrt(t2), 0.0) * (gi * x2)
    n_chunk = ch // LANES
    for c in range(n_chunk):
        for i in range(nb):
            a_s[c, i * stride:i * stride + ts, :] = a[i * ts:(i + 1) * ts, c * LANES:(c + 1) * LANES]
            b_s[c, i * stride:i * stride + ts, :] = b[i * ts:(i + 1) * ts, c * LANES:(c + 1) * LANES]

    def body(i, hs):
        t = (ts - 1 - i) if reverse else i
        rows = pl.ds(t, nb, stride=stride)
        new = []
        for c in range(n_chunk):
            hc = a_s[c, rows, :] * hs[c] + b_s[c, rows, :]
            h_s[c, rows, :] = hc
            new.append(hc)
        return tuple(new)

    h0 = tuple(carry[c] for c in range(n_chunk))
    h_end = lax.fori_loop(0, ts, body, h0, unroll=8)
    for c in range(n_chunk):
        carry[c] = h_end[c]
    return jnp.stack(
        [jnp.concatenate([h_s[c, i * stride:i * stride + ts, :] for c in range(n_chunk)], axis=-1)
         for i in range(nb)], axis=0)


def _rglru_fwd_kernel(n_lat, n_ctx, up_ref, uc_ref, un_ref, cw_ref, cb_ref, w_ref, bias_ref, lam_ref,
                      x_ref, h_ref, ubuf, a_s, b_s, h_s, carry):
    nb, ts, ch = uc_ref.shape
    step = pl.program_id(0)
    _, first, last = _scan_block(step, False, n_lat, n_ctx)

    @pl.when(step == 0)
    def _():
        carry[...] = jnp.zeros_like(carry)

    ubuf[:, SUBLANES:SUBLANES + ts, :] = uc_ref[...]
    ubuf[:, 0:SUBLANES, :] = jnp.where(first, 0.0, up_ref[...])
    ubuf[:, SUBLANES + ts:, :] = jnp.where(last, 0.0, un_ref[...])
    x = cb_ref[...][None]
    for j in range(CONV_W):
        off = SUBLANES + j - CONV_LEFT
        x = x + ubuf[:, off:off + ts, :] * cw_ref[j:j + 1, :][None]
    x_ref[...] = x
    h_ref[...] = _scan_direction(x, w_ref, bias_ref, lam_ref, a_s, b_s, h_s, carry, False)


def _rglru_bwd_kernel(x_ref, w_ref, bias_ref, lam_ref, hf_ref, g_ref, gn_ref, o_ref, a_s, b_s, h_s, carry):
    @pl.when(pl.program_id(0) == 0)
    def _():
        carry[...] = jnp.zeros_like(carry)

    h = _scan_direction(x_ref[...], w_ref, bias_ref, lam_ref, a_s, b_s, h_s, carry, True)
    g = g_ref[...]
    gelu = 0.5 * g * (1.0 + jnp.tanh(0.7978845608028654 * (g + 0.044715 * (g * g * g))))
    y = (hf_ref[...] + h) * gelu
    ms = jnp.mean(y * y, axis=-1, keepdims=True)
    o_ref[...] = (y * lax.rsqrt(ms + LN_EPS) * gn_ref[...][None]).astype(o_ref.dtype)


def _scan_scratch(nb, ts, ch):
    rows = nb * (ts + SUBLANES)
    return [pltpu.VMEM((ch // LANES, rows, LANES), _F32)] * 3 + [pltpu.VMEM((ch // LANES, nb, LANES), _F32)]


def _rglru_fwd(u, conv_w, conv_b, w_gate, b_gate, lam, n_lat, n_ctx):
    nb, r, ch = u.shape
    ts = SCAN_TILE
    halo = ts // SUBLANES
    n_halo = r // SUBLANES
    blk_of = lambda s: _scan_block(s, False, n_lat, n_ctx)[0]
    full = lambda shape: pl.BlockSpec(shape, lambda s: (0,) * len(shape))
    blk = pl.BlockSpec((nb, ts, ch), lambda s: (0, blk_of(s), 0))
    return pl.pallas_call(
        functools.partial(_rglru_fwd_kernel, n_lat, n_ctx),
        grid=(n_lat + n_ctx,),
        in_specs=[
            pl.BlockSpec((nb, SUBLANES, ch), lambda s: (0, jnp.maximum(blk_of(s) * halo - 1, 0), 0)),
            blk,
            pl.BlockSpec((nb, SUBLANES, ch), lambda s: (0, jnp.minimum((blk_of(s) + 1) * halo, n_halo - 1), 0)),
            full((CONV_W, ch)), full((1, ch)), full((ch, 2 * ch)), full((1, 2 * ch)), full((1, ch)),
        ],
        out_specs=[blk, blk],
        out_shape=[jax.ShapeDtypeStruct((nb, r, ch), _F32)] * 2,
        scratch_shapes=[pltpu.VMEM((nb, ts + 2 * SUBLANES, ch), _F32)] + _scan_scratch(nb, ts, ch),
        compiler_params=_cparams("arbitrary"),
        name="rglru_fwd",
    )(u, u, u, conv_w, conv_b, w_gate, b_gate, lam)


def _rglru_bwd(x, w_gate, b_gate, lam, hf, g, gn, n_lat, n_ctx):
    nb, r, ch = x.shape
    ts = SCAN_TILE
    full = lambda shape: pl.BlockSpec(shape, lambda s: (0,) * len(shape))
    blk = pl.BlockSpec((nb, ts, ch), lambda s: (0, _scan_block(s, True, n_lat, n_ctx)[0], 0))
    return pl.pallas_call(
        _rglru_bwd_kernel,
        grid=(n_lat + n_ctx,),
        in_specs=[blk, full((ch, 2 * ch)), full((1, 2 * ch)), full((1, ch)), blk, blk, full((1, ch))],
        out_specs=blk,
        out_shape=jax.ShapeDtypeStruct((nb, r, ch), _BF16),
        scratch_shapes=_scan_scratch(nb, ts, ch),
        compiler_params=_cparams("arbitrary"),
        name="rglru_bwd",
    )(x, w_gate, b_gate, lam, hf, g, gn)


def _attn_kernel(n_lat, sink_ref, q_ref, kp_ref, km_ref, kn_ref, kx_ref,
                 vp_ref, vm_ref, vn_ref, vx_ref, gn_ref, o_ref):
    n0 = 2 * pl.program_id(1)
    kt_mid, v_mid = km_ref[0], vm_ref[0]
    kt_lo, kt_hi = kt_mid[:, :BLK], kt_mid[:, BLK:]
    v_lo, v_hi = v_mid[:BLK], v_mid[BLK:]
    kt_x, v_x = kx_ref[0], vx_ref[0]
    for half, (kts, vs) in enumerate((((kp_ref[0], kt_lo, kt_hi), (vp_ref[0], v_lo, v_hi)),
                                      ((kt_lo, kt_hi, kn_ref[0]), (v_lo, v_hi, vn_ref[0])))):
        rows = slice(half * BLK, (half + 1) * BLK)
        kt_all = jnp.concatenate(list(kts) + [kt_x], axis=1)
        v_all = jnp.concatenate(list(vs) + [v_x], axis=0)
        out = _attend_block(n_lat, n0 + half, sink_ref, q_ref[0, rows], kt_all, v_all, gn_ref)
        for c, ch in enumerate(out):
            o_ref[0, rows, c * LANES:(c + 1) * LANES] = ch.astype(o_ref.dtype)


def _attend_block(n_lat, n, sink_ref, q, kt_all, v_all, gn_ref):
    n_keys = kt_all.shape[1]
    is_lat = n < n_lat
    lo = jnp.where(is_lat, jnp.where(n >= 1, 0, BLK), 3 * BLK)
    hi = jnp.where(is_lat, jnp.where(n + 1 < n_lat, 3 * BLK, 2 * BLK), 0)
    col = lax.broadcasted_iota(jnp.int32, (BLK, n_keys), 1)
    row = lax.broadcasted_iota(jnp.int32, (BLK, n_keys), 0)
    in_win = jnp.abs(col - BLK - row) <= WINDOW
    mask = ((col >= lo) & (col < hi) & in_win) | (col >= 3 * BLK)

    lane1 = lax.broadcasted_iota(jnp.int32, (1, LANES), 1)
    keep_lo = jnp.where(lane1 < HEAD_DIM, 1.0, 0.0).astype(_BF16)
    keep_hi = jnp.where(lane1 >= HEAD_DIM, 1.0, 0.0).astype(_BF16)
    lane_o = lax.broadcasted_iota(jnp.int32, (BLK, LANES), 1)
    chunks = []
    for hk in range(N_KV_HEADS):
        kt_h = kt_all[hk * HEAD_DIM:(hk + 1) * HEAD_DIM, :]
        kt2 = jnp.concatenate([kt_h, kt_h], axis=0)
        qs = []
        for cc in range(GQ // 2):
            c = hk * (GQ // 2) + cc
            qc = q[:, c * LANES:(c + 1) * LANES]
            qs += [qc * keep_lo, qc * keep_hi]
        s_all = _dot(jnp.concatenate(qs, axis=0), kt2)
        ps, invs = [], []
        for g in range(GQ):
            s = jnp.where(mask, s_all[g * BLK:(g + 1) * BLK], NEG_INF)
            sink = sink_ref[hk * GQ + g] * LOG2_E
            m = jnp.maximum(jnp.max(s, axis=-1, keepdims=True), sink)
            p = jnp.exp2(s - m)
            invs.append(1.0 / (jnp.sum(p, axis=-1, keepdims=True) + jnp.exp2(sink - m)))
            ps.append(p.astype(_BF16))
        o_all = _dot(jnp.concatenate(ps, axis=0), v_all)
        for cc in range(GQ // 2):
            even = o_all[(2 * cc) * BLK:(2 * cc + 1) * BLK] * invs[2 * cc]
            odd = o_all[(2 * cc + 1) * BLK:(2 * cc + 2) * BLK] * invs[2 * cc + 1]
            if hk == 0:
                odd = pltpu.roll(odd, HEAD_DIM, 1)
            else:
                even = pltpu.roll(even, HEAD_DIM, 1)
            chunks.append(jnp.where(lane_o < HEAD_DIM, even, odd))
    ssq = sum(jnp.sum(ch * ch, axis=-1, keepdims=True) for ch in chunks)
    scale = lax.rsqrt(ssq * (1.0 / D_ATTN) + LN_EPS)
    return [ch * scale * gn_ref[:, c * LANES:(c + 1) * LANES] for c, ch in enumerate(chunks)]


def _attention(q, kt, v, sink, gn, n_lat, ctx_len):
    nb, r, _ = q.shape
    pair = 2 * BLK
    assert (n_lat * BLK) % pair == 0 and ctx_len % pair == 0
    ctx_blk = (n_lat * BLK) // ctx_len
    n_pairs = n_lat // 2
    i_prev = lambda i: jnp.clip(2 * i - 1, 0, n_lat - 1)
    i_next = lambda i: jnp.clip(2 * i + 2, 0, n_lat - 1)
    i_mid = lambda i: jnp.minimum(i, n_pairs - 1)
    edge = lambda f: pl.BlockSpec((1, BLK, D_KV), lambda b, i: (b, f(i), 0))
    edge_t = lambda f: pl.BlockSpec((1, D_KV, BLK), lambda b, i: (b, 0, f(i)))
    mid = pl.BlockSpec((1, pair, D_KV), lambda b, i: (b, i_mid(i), 0))
    mid_t = pl.BlockSpec((1, D_KV, pair), lambda b, i: (b, 0, i_mid(i)))
    ctx = pl.BlockSpec((1, ctx_len, D_KV), lambda b, i: (b, ctx_blk, 0))
    ctx_t = pl.BlockSpec((1, D_KV, ctx_len), lambda b, i: (b, 0, ctx_blk))
    return pl.pallas_call(
        functools.partial(_attn_kernel, n_lat),
        grid=(nb, r // pair),
        in_specs=[
            pl.BlockSpec(memory_space=pltpu.SMEM),
            pl.BlockSpec((1, pair, D_ATTN), lambda b, i: (b, i, 0)),
            edge_t(i_prev), mid_t, edge_t(i_next), ctx_t, edge(i_prev), mid, edge(i_next), ctx,
            pl.BlockSpec((1, D_ATTN), lambda b, i: (0, 0)),
        ],
        out_specs=pl.BlockSpec((1, pair, D_ATTN), lambda b, i: (b, i, 0)),
        out_shape=jax.ShapeDtypeStruct((nb, r, D_ATTN), _BF16),
        compiler_params=_cparams("arbitrary", "arbitrary"),
        name="attention",
    )(sink, q, kt, kt, kt, kt, v, v, v, v, gn)


def _first_lane(cond, lane):
    return jnp.min(jnp.where(cond, lane, LOGIT_W), axis=-1, keepdims=True)


def _route_tile(logits, counts, lower):
    lane = lax.broadcasted_iota(jnp.int32, logits.shape, 1)
    is_group = lane < N_GROUPS
    g_max = jnp.max(jnp.where(is_group, logits, -jnp.inf), axis=-1, keepdims=True)
    g_sel = _first_lane(is_group & (logits == g_max), lane)
    p_g = 1.0 / jnp.sum(jnp.where(is_group, jnp.exp(logits - g_max), 0.0), axis=-1, keepdims=True)
    e_lo = N_GROUPS + EXP_PER_GROUP * g_sel
    in_grp = (lane >= e_lo) & (lane < e_lo + EXP_PER_GROUP)
    v1 = jnp.max(jnp.where(in_grp, logits, -jnp.inf), axis=-1, keepdims=True)
    i1 = _first_lane(in_grp & (logits == v1), lane)
    rest = in_grp & (lane != i1)
    v2 = jnp.max(jnp.where(rest, logits, -jnp.inf), axis=-1, keepdims=True)
    i2 = _first_lane(rest & (logits == v2), lane)
    e21 = jnp.exp(v2 - v1)
    gate1 = p_g / (1.0 + e21)
    gate2 = gate1 * e21
    eid1 = i1 - N_GROUPS
    eid2 = i2 - N_GROUPS
    oh1 = lane == eid1
    oh2 = lane == eid2
    onehot = jnp.where(oh1 | oh2, 1.0, 0.0)
    before = _dot(lower, onehot.astype(_BF16)) + counts
    rank1 = jnp.sum(jnp.where(oh1, before, 0.0), axis=-1, keepdims=True)
    rank2 = jnp.sum(jnp.where(oh2, before, 0.0), axis=-1, keepdims=True)
    new_counts = counts + jnp.sum(onehot, axis=0, keepdims=True)
    fields = (eid1.astype(_F32), eid2.astype(_F32), rank1, rank2, gate1, gate2)
    slab = jnp.zeros(logits.shape, _F32)
    for i, f in enumerate(fields):
        slab = jnp.where(lane == i, f, slab)
    return slab, new_counts


def _outproj_kernel(alpha, n_batch, rn_ref, an_ref, x_ref, mods_ref, wr_ref, wa_ref, lg_ref, lb_ref,
                    wrt_ref, brt_ref, x1_ref, h2_ref, lo_ref):
    for s in range(x_ref.shape[1] // ROW_TILE):
        rows = slice(s * ROW_TILE, (s + 1) * ROW_TILE)
        m = _sub_tile_mod(s, x_ref.shape[1] // ROW_TILE, n_batch)
        y = _dot(rn_ref[0, rows], wr_ref[...]) + _dot(an_ref[0, rows], wa_ref[...])
        z = alpha * x_ref[0, rows] + mods_ref[m, 2:3, :] * y
        x1 = _normalize(z) * lg_ref[...] + lb_ref[...]
        x1_ref[0, rows] = x1
        h2 = _normalize(x1) * (1.0 + mods_ref[m, 4:5, :]) + mods_ref[m, 3:4, :]
        _store_row_tiles(h2_ref, s * ROW_TILE, h2)
        lo_ref[0, rows] = _dot(h2.astype(_BF16), wrt_ref[...]) + brt_ref[...]


def _outproj(rn, an, x, mods, w_out, ln_g, ln_b, w_rt, b_rt, alpha):
    nb, r, d = x.shape
    tm = ROW_TILE * ROW_GROUP
    row = lambda w: pl.BlockSpec((1, tm, w), lambda b, j: (b, j, 0))
    full = lambda shape: pl.BlockSpec(shape, lambda b, j: (0,) * len(shape))
    return pl.pallas_call(
        functools.partial(_outproj_kernel, alpha, nb),
        grid=(nb, r // tm),
        in_specs=[
            row(D_RNN), row(D_ATTN), row(d), full(mods.shape),
            pl.BlockSpec((D_RNN, d), lambda b, j: (0, 0)),
            pl.BlockSpec((D_ATTN, d), lambda b, j: (1, 0)),
            full((1, d)), full((1, d)), full((d, LOGIT_W)), full((1, LOGIT_W)),
        ],
        out_specs=[row(d), pl.BlockSpec((tm * ROW_TILE_SUB, LANES), lambda b, j: (b * (r // tm) + j, 0)),
                   row(LOGIT_W)],
        out_shape=[
            jax.ShapeDtypeStruct((nb, r, d), _F32),
            jax.ShapeDtypeStruct((nb * r * ROW_TILE_SUB, LANES), _F32),
            jax.ShapeDtypeStruct((nb, r, LOGIT_W), _F32),
        ],
        compiler_params=_cparams("arbitrary", "arbitrary"),
        name="outproj",
    )(rn, an, x, mods, w_out, w_out, ln_g, ln_b, w_rt, b_rt)


def _route_kernel(lg_ref, low_ref, rt_ref, cnt_ref, cnt_s):
    @pl.when(pl.program_id(0) == 0)
    def _():
        cnt_s[...] = jnp.zeros_like(cnt_s)

    slab, counts = _route_tile(lg_ref[...], cnt_s[...], low_ref[...])
    rt_ref[...] = slab
    cnt_s[...] = counts
    cnt_ref[...] = counts


def _route_slab(logits):
    nt = logits.shape[0]
    tm = next(t for t in ROUTE_TILES if nt % t == 0)
    lower = jnp.tril(jnp.ones((tm, tm), _BF16), -1)
    return pl.pallas_call(
        _route_kernel,
        grid=(nt // tm,),
        in_specs=[pl.BlockSpec((tm, LOGIT_W), lambda i: (i, 0)), pl.BlockSpec((tm, tm), lambda i: (0, 0))],
        out_specs=[pl.BlockSpec((tm, LOGIT_W), lambda i: (i, 0)), pl.BlockSpec((1, LOGIT_W), lambda i: (0, 0))],
        out_shape=[jax.ShapeDtypeStruct((nt, LOGIT_W), _F32), jax.ShapeDtypeStruct((1, LOGIT_W), _F32)],
        scratch_shapes=[pltpu.VMEM((1, LOGIT_W), _F32)],
        compiler_params=_cparams("arbitrary"),
        name="route",
    )(logits, lower)


def _tile_rows(i):
    return pl.ds(pl.multiple_of(i * ROW_TILE_SUB, ROW_TILE_SUB), ROW_TILE_SUB)


def _row_gather(idx_ref, n_rows, src_hbm, dst, dst_base, sem):
    def body(i, carry):
        t = idx_ref[0, 0, i]
        pltpu.make_async_copy(src_hbm.at[_tile_rows(t), :], dst.at[_tile_rows(dst_base + i), :], sem).start()
        return carry

    lax.fori_loop(0, n_rows, body, 0, unroll=8)


def _row_gather_static(idx_ref, first, stop, src_hbm, dst, dst_base, sem):
    for i in range(first, stop):
        t = idx_ref[0, 0, i]
        pltpu.make_async_copy(src_hbm.at[_tile_rows(t), :], dst.at[_tile_rows(dst_base + i), :], sem).start()


def _gather_wait(n_rows, src_hbm, dst, dst_base, sem):
    pltpu.make_async_copy(src_hbm.at[pl.ds(0, n_rows * ROW_TILE_SUB), :],
                          dst.at[pl.ds(pl.multiple_of(dst_base * ROW_TILE_SUB, ROW_TILE_SUB),
                                       n_rows * ROW_TILE_SUB), :], sem).wait()


def _moe_kernel(blk_e_ref, tok_ref, tok_next_ref, h_hbm, w1_ref, w3_ref, w2_ref, o_ref,
                xbuf, sem, w1_s, w3_s, w2_s):
    j = pl.program_id(0)
    n_used = blk_e_ref[pl.num_programs(0)]
    slot = j % 2
    bm = tok_ref.shape[2]

    last = pl.num_programs(0) - 1

    @pl.when(j == 0)
    def _():
        _row_gather(tok_ref, bm, h_hbm, xbuf, 0, sem.at[0])

    @pl.when(j < n_used)
    def _():
        @pl.when((j == 0) | (blk_e_ref[j] != blk_e_ref[jnp.maximum(j - 1, 0)]))
        def _():
            w1_s[...] = w1_ref[0, 0].astype(_BF16)
            w3_s[...] = w3_ref[0, 0].astype(_BF16)
            w2_s[...] = w2_ref[0, 0].astype(_BF16)

        _gather_wait(bm, h_hbm, xbuf, slot * bm, sem.at[slot])
        x = _load_row_tiles(xbuf, slot * bm, bm).astype(_BF16)
        third = bm // 3
        _row_gather_static(tok_next_ref, 0, third, h_hbm, xbuf, (1 - slot) * bm, sem.at[1 - slot])
        a = _dot(x, w1_s[...])
        _row_gather_static(tok_next_ref, third, 2 * third, h_hbm, xbuf, (1 - slot) * bm, sem.at[1 - slot])
        h = (a * _sigmoid(a)) * _dot(x, w3_s[...])
        _row_gather_static(tok_next_ref, 2 * third, bm, h_hbm, xbuf, (1 - slot) * bm, sem.at[1 - slot])
        _store_row_tiles(o_ref, 0, _dot(h.astype(_BF16), w2_s[...]))

        @pl.when(j == last)
        def _():
            _gather_wait(bm, h_hbm, xbuf, (1 - slot) * bm, sem.at[1 - slot])

    @pl.when(j >= n_used)
    def _():
        @pl.when(j == n_used)
        def _():
            _gather_wait(bm, h_hbm, xbuf, slot * bm, sem.at[slot])

        o_ref[...] = jnp.zeros_like(o_ref)


def _moe_ffn(h2, slot_tok, blk_e, w1, w3, w2, layer):
    d = D_MODEL
    bm = MOE_TILE
    n_blk = blk_e.shape[0] - 1
    tok3 = slot_tok.reshape(n_blk, 1, bm)
    smem_blk = lambda f: pl.BlockSpec((1, 1, bm), lambda j, be: (f(j), 0, 0), memory_space=pltpu.SMEM)
    grid_spec = pltpu.PrefetchScalarGridSpec(
        num_scalar_prefetch=1,
        grid=(n_blk,),
        in_specs=[
            smem_blk(lambda j: j),
            smem_blk(lambda j: jnp.minimum(j + 1, n_blk - 1)),
            pl.BlockSpec(memory_space=pl.ANY),
            pl.BlockSpec((1, 1, d, D_EXPERT), lambda j, be: (layer, be[j], 0, 0)),
            pl.BlockSpec((1, 1, d, D_EXPERT), lambda j, be: (layer, be[j], 0, 0)),
            pl.BlockSpec((1, 1, D_EXPERT, d), lambda j, be: (layer, be[j], 0, 0)),
        ],
        out_specs=pl.BlockSpec((bm * ROW_TILE_SUB, LANES), lambda j, be: (j, 0)),
        scratch_shapes=[
            pltpu.VMEM((2 * bm * ROW_TILE_SUB, LANES), _F32), pltpu.SemaphoreType.DMA((2,)),
            pltpu.VMEM((d, D_EXPERT), _BF16), pltpu.VMEM((d, D_EXPERT), _BF16),
            pltpu.VMEM((D_EXPERT, d), _BF16),
        ],
    )
    return pl.pallas_call(
        _moe_kernel,
        grid_spec=grid_spec,
        out_shape=jax.ShapeDtypeStruct((n_blk * bm * ROW_TILE_SUB, LANES), _F32),
        compiler_params=_cparams("arbitrary"),
        name="moe_ffn",
    )(blk_e, tok3, tok3, h2, w1, w3, w2)


def _combine_kernel(alpha, dst_ref, dst_next_ref, y_hbm, gate_ref, x_ref, mod_ref, lg_ref, lb_ref,
                    o_ref, ybuf, sem):
    s = pl.program_id(0)
    n = pl.num_programs(0)
    slot = s % 2
    tm = x_ref.shape[0]

    nr = TOP_K * tm

    @pl.when(s == 0)
    def _():
        _row_gather(dst_ref, nr, y_hbm, ybuf, 0, sem.at[0])

    _gather_wait(nr, y_hbm, ybuf, slot * nr, sem.at[slot])
    nxt = (1 - slot) * nr
    gate = gate_ref[...]
    y0 = _load_row_tiles(ybuf, slot * nr, tm) * gate[:, 0:1]
    _row_gather_static(dst_next_ref, 0, nr // 2, y_hbm, ybuf, nxt, sem.at[1 - slot])
    y = y0 + _load_row_tiles(ybuf, slot * nr + tm, tm) * gate[:, 1:2]
    z = alpha * x_ref[...] + mod_ref[0, 5:6, :] * y
    _row_gather_static(dst_next_ref, nr // 2, nr, y_hbm, ybuf, nxt, sem.at[1 - slot])
    o_ref[...] = _normalize(z) * lg_ref[...] + lb_ref[...]

    @pl.when(s == n - 1)
    def _():
        _gather_wait(nr, y_hbm, ybuf, nxt, sem.at[1 - slot])


def _combine(yb, dest, gate, x1, mods, ln_g, ln_b, n_batch, tiles_per_batch, tiles_used, alpha):
    nt, d = x1.shape
    tm = ROW_TILE
    n_steps = n_batch * tiles_used
    dst3 = dest.reshape(nt // tm, tm, TOP_K).transpose(0, 2, 1).reshape(nt // tm, 1, TOP_K * tm)

    def tile(s):
        return (s // tiles_used) * tiles_per_batch + s % tiles_used

    def mod_idx(s):
        return (jnp.where(s % tiles_used >= tiles_per_batch - 1, n_batch, s // tiles_used), 0, 0)

    smem_blk = lambda f: pl.BlockSpec((1, 1, TOP_K * tm), lambda s: (tile(f(s)), 0, 0),
                                      memory_space=pltpu.SMEM)
    full = lambda shape: pl.BlockSpec(shape, lambda s: (0,) * len(shape))
    return pl.pallas_call(
        functools.partial(_combine_kernel, alpha),
        grid=(n_steps,),
        in_specs=[
            smem_blk(lambda s: s),
            smem_blk(lambda s: jnp.minimum(s + 1, n_steps - 1)),
            pl.BlockSpec(memory_space=pl.ANY),
            pl.BlockSpec((tm, TOP_K), lambda s: (tile(s), 0)),
            pl.BlockSpec((tm, d), lambda s: (tile(s), 0)),
            pl.BlockSpec((1, 6, d), mod_idx),
            full((1, d)), full((1, d)),
        ],
        out_specs=pl.BlockSpec((tm, d), lambda s: (s, 0)),
        out_shape=jax.ShapeDtypeStruct((n_steps * tm, d), _F32),
        scratch_shapes=[pltpu.VMEM((2 * TOP_K * tm * ROW_TILE_SUB, LANES), _F32),
                        pltpu.SemaphoreType.DMA((2,))],
        compiler_params=_cparams("arbitrary"),
        name="combine_ln2",
    )(dst3, dst3, yb, gate, x1, mods, ln_g, ln_b)


def _route(slab, counts, n_blk):
    nt = slab.shape[0]
    bm = MOE_TILE
    eid = slab[:, 0:TOP_K].astype(jnp.int32)
    rank = slab[:, TOP_K:2 * TOP_K].astype(jnp.int32)
    gate = slab[:, 2 * TOP_K:3 * TOP_K]
    counts = counts[0, :N_EXPERTS].astype(jnp.int32)
    experts = jnp.arange(N_EXPERTS, dtype=jnp.int32)
    pcounts = (counts + bm - 1) // bm * bm
    pends = jnp.cumsum(pcounts)
    pstarts = pends - pcounts
    dest = jnp.sum(jnp.where(eid[:, :, None] == experts, pstarts, 0), axis=-1) + rank
    n_assign = nt * TOP_K
    n_fill = n_blk * bm - n_assign
    pad_ends = jnp.cumsum(pcounts - counts)
    fill_key = jnp.sum(pad_ends[None, :] <= jnp.arange(n_fill, dtype=jnp.int32)[:, None], axis=-1,
                       dtype=jnp.int32)
    keys = jnp.concatenate([eid.reshape(-1), fill_key])
    vals = jnp.concatenate([jnp.arange(n_assign, dtype=jnp.int32) // TOP_K, jnp.zeros((n_fill,), jnp.int32)])
    _, slot_tok = lax.sort_key_val(keys, vals)
    blk_start = jnp.arange(n_blk, dtype=jnp.int32) * bm
    blk_e = jnp.minimum(jnp.sum(pends[None, :] <= blk_start[:, None], axis=-1, dtype=jnp.int32), N_EXPERTS - 1)
    blk_meta = jnp.concatenate([blk_e, pends[-1:] // bm]).astype(jnp.int32)
    return dest, gate, slot_tok, blk_meta


def _rope_tables(seq, ident_rows):
    rows = seq // GRID_W
    row = jnp.repeat(jnp.arange(rows, dtype=_F32), GRID_W)
    col = jnp.tile(jnp.arange(GRID_W, dtype=_F32), rows)
    inv = ROPE_BASE ** (-jnp.arange(0, AXIS_DIM, 2, dtype=_F32) / AXIS_DIM)
    ang_r = row[:, None] * inv
    ang_c = col[:, None] * inv
    cr, sr, cc, sc = jnp.cos(ang_r), jnp.sin(ang_r), jnp.cos(ang_c), jnp.sin(ang_c)
    z = jnp.zeros_like(sr)
    reps = LANES // HEAD_DIM
    cos = jnp.tile(jnp.concatenate([cr, cr, cc, cc], -1), (1, reps))
    sin_a = jnp.tile(jnp.concatenate([-sr, z, -sc, z], -1), (1, reps))
    sin_b = jnp.tile(jnp.concatenate([z, sr, z, sc], -1), (1, reps))
    one = jnp.ones((ident_rows, LANES), _F32)
    zero = jnp.zeros((ident_rows, LANES), _F32)
    return (jnp.concatenate([cos, one], 0), jnp.concatenate([sin_a, zero], 0),
            jnp.concatenate([sin_b, zero], 0))


def _block_diag(w):
    eye = jnp.eye(RG_BLOCKS, dtype=w.dtype)
    return jnp.einsum("hij,hg->higj", w, eye).reshape(D_RNN, D_RNN)


def kernel(x, c, ctx, c_ctx, w_mod, b_mod, w_in, conv_w, conv_b, rg_wa, rg_ba, rg_wx, rg_bx, rg_lam,
           attn_sink, gn_rnn, gn_attn, w_out, ln1_g, ln1_b, router_g, router_gb, router_e, router_eb,
           w1, w3, w2, ln2_g, ln2_b):
    nb, seq, d = x.shape
    ctx_len = ctx.shape[1]
    depth = w_mod.shape[0]
    r = seq + ctx_len
    nt = nb * r
    alpha = (2 * depth) ** 0.25
    assert d == D_MODEL and seq % ROW_TILE == 0 and ctx_len % ROW_TILE == 0
    assert seq % GRID_W == 0 and seq % ctx_len == 0 and ctx_len % BLK == 0
    n_lat_tiles = seq // ROW_TILE
    tiles_per_batch = r // ROW_TILE
    assert tiles_per_batch == n_lat_tiles + 1, "one context tile per batch"
    assert tiles_per_batch % ROW_GROUP == 0
    n_blk = (nt * TOP_K + N_EXPERTS * (MOE_TILE - 1) + MOE_TILE - 1) // MOE_TILE

    pad_rows = (-(nb + 1)) % SUBLANES
    c_all = jnp.concatenate([c, c_ctx[None, :], jnp.zeros((pad_rows, d), _F32)], 0)
    mods_all = _modulation(c_all, w_mod, b_mod).reshape(depth, nb + 1 + pad_rows, 6, d)

    rope = _rope_tables(seq, ROW_TILE)
    xs = jnp.concatenate([x, ctx], axis=1)

    out = None
    for l in range(depth):
        last = l == depth - 1
        mods = mods_all[l]
        u, g, q, k, v = _inproj(xs, mods, w_in[l].astype(_BF16), rope)

        n_lat_s, n_ctx_s = seq // SCAN_TILE, ctx_len // SCAN_TILE
        gate_w = [(0.5 * jnp.concatenate([_block_diag(rg_wa[l, dr]), _block_diag(rg_wx[l, dr])], 1)).astype(_BF16)
                  for dr in range(2)]
        gate_b = [0.5 * jnp.concatenate([rg_ba[l, dr], rg_bx[l, dr]])[None, :] for dr in range(2)]
        xc, hf = _rglru_fwd(u, conv_w[l], conv_b[l][None, :], gate_w[0], gate_b[0], rg_lam[l, 0][None, :],
                            n_lat_s, n_ctx_s)
        rn = _rglru_bwd(xc, gate_w[1], gate_b[1], rg_lam[l, 1][None, :], hf, g, gn_rnn[l][None, :],
                        n_lat_s, n_ctx_s)

        an = _attention(q, k, v, attn_sink[l], gn_attn[l][None, :], seq // BLK, ctx_len)

        w_rt = jnp.concatenate([router_g[l], router_e[l],
                                jnp.zeros((d, LOGIT_W - N_GROUPS - N_EXPERTS), _F32)], 1).astype(_BF16)
        b_rt = jnp.concatenate([router_gb[l], router_eb[l],
                                jnp.zeros((LOGIT_W - N_GROUPS - N_EXPERTS,), _F32)])[None, :]
        x1, h2, logits = _outproj(rn, an, xs, mods, w_out[l].astype(_BF16), ln1_g[l][None, :],
                                  ln1_b[l][None, :], w_rt, b_rt, alpha)
        slab, counts = _route_slab(logits.reshape(nt, LOGIT_W))
        dest, gate, slot_tok, blk_e = _route(slab, counts, n_blk)
        yb = _moe_ffn(h2, slot_tok, blk_e, w1, w3, w2, l)
        tiles_used = n_lat_tiles if last else tiles_per_batch
        x2 = _combine(yb, dest, gate, x1.reshape(nt, d), mods, ln2_g[l][None, :], ln2_b[l][None, :],
                      nb, tiles_per_batch, tiles_used, alpha)
        if last:
            out = x2.reshape(nb, seq, d)
        else:
            xs = x2.reshape(nb, r, d)
    return out
```

```python
import functools

import jax
import jax.numpy as jnp
from jax import lax
from jax.experimental import pallas as pl
from jax.experimental.pallas import tpu as pltpu

D_MODEL = 1024
D_RNN = 512
D_ATTN = 512
HEAD_DIM = 64
N_Q_HEADS = 8
N_KV_HEADS = 2
GQ = N_Q_HEADS // N_KV_HEADS
D_KV = N_KV_HEADS * HEAD_DIM
WINDOW = 128
BLK = 128
ATTN_SCALE = HEAD_DIM ** -0.5
AXIS_DIM = HEAD_DIM // 2
ROPE_BASE = 10000.0
GRID_W = 64
RG_BLOCKS = 8
RG_BD = D_RNN // RG_BLOCKS
RG_C = 8.0
CONV_W = 4
CONV_LEFT = 2
N_GROUPS = 4
EXP_PER_GROUP = 8
N_EXPERTS = N_GROUPS * EXP_PER_GROUP
TOP_K = 2
D_EXPERT = 512
D_IN = 2 * D_RNN + D_ATTN + 2 * D_KV
LN_EPS = 1e-6
NEG_INF = -1e30

LANES = 128
SUBLANES = 8
LOG2_E = 1.4426950408889634
Q_SCALE = ATTN_SCALE * LOG2_E
ROW_TILE = 256
ROW_GROUP = 3
ROUTE_TILES = (1024, 512, 256)
SCAN_TILE = 128
MOE_TILE = 512
LOGIT_W = LANES
VMEM_LIMIT = 56 * 1024 * 1024

_F32 = jnp.float32
_BF16 = jnp.bfloat16


def _cparams(*sem):
    return pltpu.CompilerParams(dimension_semantics=sem, vmem_limit_bytes=VMEM_LIMIT)


def _sigmoid(x):
    return 1.0 / (1.0 + jnp.exp(-x))


def _dot(a, b):
    return jnp.dot(a, b, preferred_element_type=_F32)


def _mod_kernel(c_ref, w_ref, b_ref, o_ref):
    c = c_ref[...]
    a = c * _sigmoid(c)
    a_hi = a.astype(_BF16)
    a_lo = (a - a_hi.astype(_F32)).astype(_BF16)
    w = w_ref[0]
    w_hi = w.astype(_BF16)
    w_lo = (w - w_hi.astype(_F32)).astype(_BF16)
    o_ref[0] = _dot(a_hi, w_hi) + (_dot(a_lo, w_hi) + _dot(a_hi, w_lo)) + b_ref[0]


def _modulation(c_all, w_mod, b_mod):
    depth, d, n = w_mod.shape
    rows = c_all.shape[0]
    tn = 1536
    return pl.pallas_call(
        _mod_kernel,
        grid=(depth, n // tn),
        in_specs=[
            pl.BlockSpec((rows, d), lambda l, j: (0, 0)),
            pl.BlockSpec((1, d, tn), lambda l, j: (l, 0, j)),
            pl.BlockSpec((1, 1, tn), lambda l, j: (l, 0, j)),
        ],
        out_specs=pl.BlockSpec((1, rows, tn), lambda l, j: (l, 0, j)),
        out_shape=jax.ShapeDtypeStruct((depth, rows, n), _F32),
        compiler_params=_cparams("arbitrary", "arbitrary"),
        name="modulation",
    )(c_all, w_mod, b_mod.reshape(depth, 1, n))


def _normalize(x):
    mu = jnp.mean(x, axis=-1, keepdims=True)
    xc = x - mu
    var = jnp.mean(xc * xc, axis=-1, keepdims=True)
    return xc * lax.rsqrt(var + LN_EPS)


def _rope128(x, cos, sin_a, sin_b):
    return (x * cos + pltpu.roll(x, LANES - AXIS_DIM // 2, 1) * sin_a
            + pltpu.roll(x, AXIS_DIM // 2, 1) * sin_b)


ROW_TILE_SUB = D_MODEL // LANES


def _store_row_tiles(ref, base, x):
    rows = x.shape[0]
    for s in range(ROW_TILE_SUB):
        ref[pl.ds(base * ROW_TILE_SUB + s, rows, stride=ROW_TILE_SUB), :] = x[:, s * LANES:(s + 1) * LANES]


def _load_row_tiles(ref, base, rows):
    return jnp.concatenate(
        [ref[pl.ds(base * ROW_TILE_SUB + s, rows, stride=ROW_TILE_SUB), :] for s in range(ROW_TILE_SUB)],
        axis=-1)


def _mod_index(b, j, n_lat_tiles, n_batch):
    return jnp.where(j >= n_lat_tiles, n_batch, b)


def _sub_tile_mod(s, n_sub, n_batch):
    b = pl.program_id(0)
    if s != n_sub - 1:
        return b
    return jnp.where(pl.program_id(1) == pl.num_programs(1) - 1, n_batch, b)


def _inproj_kernel(n_batch, x_ref, mods_ref, w_ref, cos_ref, sa_ref, sb_ref,
                   u_ref, g_ref, q_ref, kt_ref, v_ref):
    for s in range(x_ref.shape[1] // ROW_TILE):
        rows = slice(s * ROW_TILE, (s + 1) * ROW_TILE)
        m = _sub_tile_mod(s, x_ref.shape[1] // ROW_TILE, n_batch)
        h = _normalize(x_ref[0, rows]) * (1.0 + mods_ref[m, 1:2, :]) + mods_ref[m, 0:1, :]
        y = _dot(h.astype(_BF16), w_ref[...])
        u_ref[0, rows] = y[:, :D_RNN]
        g_ref[0, rows] = y[:, D_RNN:2 * D_RNN]
        cos, sa, sb = cos_ref[rows], sa_ref[rows], sb_ref[rows]
        q0 = 2 * D_RNN
        for c in range(D_ATTN // LANES):
            qc = _rope128(y[:, q0 + c * LANES:q0 + (c + 1) * LANES], cos, sa, sb)
            q_ref[0, rows, c * LANES:(c + 1) * LANES] = (qc * Q_SCALE).astype(_BF16)
        k0 = q0 + D_ATTN
        kt_ref[0, :, rows] = _rope128(y[:, k0:k0 + D_KV], cos, sa, sb).T.astype(_BF16)
        v_ref[0, rows] = y[:, k0 + D_KV:k0 + 2 * D_KV].astype(_BF16)


def _inproj(x, mods, w_in, rope):
    nb, r, d = x.shape
    tm = ROW_TILE * ROW_GROUP
    nj = r // tm
    cos, sa, sb = rope
    rope_spec = pl.BlockSpec((tm, LANES), lambda b, j: (j, 0))
    row = lambda w: pl.BlockSpec((1, tm, w), lambda b, j: (b, j, 0))
    return pl.pallas_call(
        functools.partial(_inproj_kernel, nb),
        grid=(nb, nj),
        in_specs=[
            row(d),
            pl.BlockSpec(mods.shape, lambda b, j: (0, 0, 0)),
            pl.BlockSpec((d, D_IN), lambda b, j: (0, 0)),
            rope_spec, rope_spec, rope_spec,
        ],
        out_specs=[row(D_RNN), row(D_RNN), row(D_ATTN),
                   pl.BlockSpec((1, D_KV, tm), lambda b, j: (b, 0, j)), row(D_KV)],
        out_shape=[
            jax.ShapeDtypeStruct((nb, r, D_RNN), _F32),
            jax.ShapeDtypeStruct((nb, r, D_RNN), _F32),
            jax.ShapeDtypeStruct((nb, r, D_ATTN), _BF16),
            jax.ShapeDtypeStruct((nb, D_KV, r), _BF16),
            jax.ShapeDtypeStruct((nb, r, D_KV), _BF16),
        ],
        compiler_params=_cparams("arbitrary", "arbitrary"),
        name="inproj",
    )(x, mods, w_in, cos, sa, sb)


def _scan_block(step, reverse, n_lat, n_ctx):
    in_ctx = step < n_ctx
    i_ctx = (n_ctx - 1 - step) if reverse else step
    s_lat = step - n_ctx
    i_lat = (n_lat - 1 - s_lat) if reverse else s_lat
    blk = jnp.where(in_ctx, n_lat + i_ctx, i_lat)
    first = jnp.where(in_ctx, i_ctx == 0, i_lat == 0)
    last = jnp.where(in_ctx, i_ctx == n_ctx - 1, i_lat == n_lat - 1)
    return blk, first, last


def _scan_direction(x, w_ref, bias_ref, lam_ref, a_s, b_s, h_s, carry, reverse):
    nb, ts, ch = x.shape
    stride = a_s.shape[1] // nb
    x2 = x.reshape(nb * ts, ch)
    th = jnp.tanh(_dot(x2.astype(_BF16), w_ref[...]) + bias_ref[...])
    gi = 0.5 * th[:, ch:] + 0.5
    neg_lam = -lam_ref[...]
    softplus = jnp.maximum(neg_lam, 0.0) + jnp.log1p(jnp.exp(-jnp.abs(neg_lam)))
    c2 = (-0.5 * RG_C * LOG2_E) * softplus
    log2_a = c2 * th[:, :ch] + c2
    a = jnp.exp2(log2_a)
    t2 = jnp.tanh(log2_a * (-1.0 / LOG2_E)) * (1.0 + a * a)
    b = jnp.where(t2 > 0.0, t2 * lax.rsqrt(t2), 0.0) * (gi * x2)
    n_chunk = ch // LANES
    for c in range(n_chunk):
        for i in range(nb):
            a_s[c, i * stride:i * stride + ts, :] = a[i * ts:(i + 1) * ts, c * LANES:(c + 1) * LANES]
            b_s[c, i * stride:i * stride + ts, :] = b[i * ts:(i + 1) * ts, c * LANES:(c + 1) * LANES]

    def body(i, hs):
        t = (ts - 1 - i) if reverse else i
        rows = pl.ds(t, nb, stride=stride)
        new = []
        for c in range(n_chunk):
            hc = a_s[c, rows, :] * hs[c] + b_s[c, rows, :]
            h_s[c, rows, :] = hc
            new.append(hc)
        return tuple(new)

    h0 = tuple(carry[c] for c in range(n_chunk))
    h_end = lax.fori_loop(0, ts, body, h0, unroll=8)
    for c in range(n_chunk):
        carry[c] = h_end[c]
    return jnp.stack(
        [jnp.concatenate([h_s[c, i * stride:i * stride + ts, :] for c in range(n_chunk)], axis=-1)
         for i in range(nb)], axis=0)


def _rglru_fwd_kernel(n_lat, n_ctx, up_ref, uc_ref, un_ref, cw_ref, cb_ref, w_ref, bias_ref, lam_ref,
                      x_ref, h_ref, ubuf, a_s, b_s, h_s, carry):
    nb, ts, ch = uc_ref.shape
    step = pl.program_id(0)
    _, first, last = _scan_block(step, False, n_lat, n_ctx)

    @pl.when(step == 0)
    def _():
        carry[...] = jnp.zeros_like(carry)

    ubuf[:, SUBLANES:SUBLANES + ts, :] = uc_ref[...]
    ubuf[:, 0:SUBLANES, :] = jnp.where(first, 0.0, up_ref[...])
    ubuf[:, SUBLANES + ts:, :] = jnp.where(last, 0.0, un_ref[...])
    x = cb_ref[...][None]
    for j in range(CONV_W):
        off = SUBLANES + j - CONV_LEFT
        x = x + ubuf[:, off:off + ts, :] * cw_ref[j:j + 1, :][None]
    x_ref[...] = x
    h_ref[...] = _scan_direction(x, w_ref, bias_ref, lam_ref, a_s, b_s, h_s, carry, False)


def _rglru_bwd_kernel(x_ref, w_ref, bias_ref, lam_ref, hf_ref, g_ref, gn_ref, o_ref, a_s, b_s, h_s, carry):
    @pl.when(pl.program_id(0) == 0)
    def _():
        carry[...] = jnp.zeros_like(carry)

    h = _scan_direction(x_ref[...], w_ref, bias_ref, lam_ref, a_s, b_s, h_s, carry, True)
    g = g_ref[...]
    gelu = 0.5 * g * (1.0 + jnp.tanh(0.7978845608028654 * (g + 0.044715 * (g * g * g))))
    y = (hf_ref[...] + h) * gelu
    ms = jnp.mean(y * y, axis=-1, keepdims=True)
    o_ref[...] = (y * lax.rsqrt(ms + LN_EPS) * gn_ref[...][None]).astype(o_ref.dtype)


def _scan_scratch(nb, ts, ch):
    rows = nb * (ts + SUBLANES)
    return [pltpu.VMEM((ch // LANES, rows, LANES), _F32)] * 3 + [pltpu.VMEM((ch // LANES, nb, LANES), _F32)]


def _rglru_fwd(u, conv_w, conv_b, w_gate, b_gate, lam, n_lat, n_ctx):
    nb, r, ch = u.shape
    ts = SCAN_TILE
    halo = ts // SUBLANES
    n_halo = r // SUBLANES
    blk_of = lambda s: _scan_block(s, False, n_lat, n_ctx)[0]
    full = lambda shape: pl.BlockSpec(shape, lambda s: (0,) * len(shape))
    blk = pl.BlockSpec((nb, ts, ch), lambda s: (0, blk_of(s), 0))
    return pl.pallas_call(
        functools.partial(_rglru_fwd_kernel, n_lat, n_ctx),
        grid=(n_lat + n_ctx,),
        in_specs=[
            pl.BlockSpec((nb, SUBLANES, ch), lambda s: (0, jnp.maximum(blk_of(s) * halo - 1, 0), 0)),
            blk,
            pl.BlockSpec((nb, SUBLANES, ch), lambda s: (0, jnp.minimum((blk_of(s) + 1) * halo, n_halo - 1), 0)),
            full((CONV_W, ch)), full((1, ch)), full((ch, 2 * ch)), full((1, 2 * ch)), full((1, ch)),
        ],
        out_specs=[blk, blk],
        out_shape=[jax.ShapeDtypeStruct((nb, r, ch), _F32)] * 2,
        scratch_shapes=[pltpu.VMEM((nb, ts + 2 * SUBLANES, ch), _F32)] + _scan_scratch(nb, ts, ch),
        compiler_params=_cparams("arbitrary"),
        name="rglru_fwd",
    )(u, u, u, conv_w, conv_b, w_gate, b_gate, lam)


def _rglru_bwd(x, w_gate, b_gate, lam, hf, g, gn, n_lat, n_ctx):
    nb, r, ch = x.shape
    ts = SCAN_TILE
    full = lambda shape: pl.BlockSpec(shape, lambda s: (0,) * len(shape))
    blk = pl.BlockSpec((nb, ts, ch), lambda s: (0, _scan_block(s, True, n_lat, n_ctx)[0], 0))
    return pl.pallas_call(
        _rglru_bwd_kernel,
        grid=(n_lat + n_ctx,),
        in_specs=[blk, full((ch, 2 * ch)), full((1, 2 * ch)), full((1, ch)), blk, blk, full((1, ch))],
        out_specs=blk,
        out_shape=jax.ShapeDtypeStruct((nb, r, ch), _BF16),
        scratch_shapes=_scan_scratch(nb, ts, ch),
        compiler_params=_cparams("arbitrary"),
        name="rglru_bwd",
    )(x, w_gate, b_gate, lam, hf, g, gn)


def _attn_kernel(n_lat, sink_ref, q_ref, kp_ref, km_ref, kn_ref, kx_ref,
                 vp_ref, vm_ref, vn_ref, vx_ref, gn_ref, o_ref):
    n0 = 2 * pl.program_id(1)
    kt_mid, v_mid = km_ref[0], vm_ref[0]
    kt_lo, kt_hi = kt_mid[:, :BLK], kt_mid[:, BLK:]
    v_lo, v_hi = v_mid[:BLK], v_mid[BLK:]
    kt_x, v_x = kx_ref[0], vx_ref[0]
    for half, (kts, vs) in enumerate((((kp_ref[0], kt_lo, kt_hi), (vp_ref[0], v_lo, v_hi)),
                                      ((kt_lo, kt_hi, kn_ref[0]), (v_lo, v_hi, vn_ref[0])))):
        rows = slice(half * BLK, (half + 1) * BLK)
        kt_all = jnp.concatenate(list(kts) + [kt_x], axis=1)
        v_all = jnp.concatenate(list(vs) + [v_x], axis=0)
        out = _attend_block(n_lat, n0 + half, sink_ref, q_ref[0, rows], kt_all, v_all, gn_ref)
        for c, ch in enumerate(out):
            o_ref[0, rows, c * LANES:(c + 1) * LANES] = ch.astype(o_ref.dtype)


def _attend_block(n_lat, n, sink_ref, q, kt_all, v_all, gn_ref):
    n_keys = kt_all.shape[1]
    is_lat = n < n_lat
    lo = jnp.where(is_lat, jnp.where(n >= 1, 0, BLK), 3 * BLK)
    hi = jnp.where(is_lat, jnp.where(n + 1 < n_lat, 3 * BLK, 2 * BLK), 0)
    col = lax.broadcasted_iota(jnp.int32, (BLK, n_keys), 1)
    row = lax.broadcasted_iota(jnp.int32, (BLK, n_keys), 0)
    in_win = jnp.abs(col - BLK - row) <= WINDOW
    mask = ((col >= lo) & (col < hi) & in_win) | (col >= 3 * BLK)

    lane1 = lax.broadcasted_iota(jnp.int32, (1, LANES), 1)
    keep_lo = jnp.where(lane1 < HEAD_DIM, 1.0, 0.0).astype(_BF16)
    keep_hi = jnp.where(lane1 >= HEAD_DIM, 1.0, 0.0).astype(_BF16)
    lane_o = lax.broadcasted_iota(jnp.int32, (BLK, LANES), 1)
    chunks = []
    for hk in range(N_KV_HEADS):
        kt_h = kt_all[hk * HEAD_DIM:(hk + 1) * HEAD_DIM, :]
        kt2 = jnp.concatenate([kt_h, kt_h], axis=0)
        qs = []
        for cc in range(GQ // 2):
            c = hk * (GQ // 2) + cc
            qc = q[:, c * LANES:(c + 1) * LANES]
            qs += [qc * keep_lo, qc * keep_hi]
        s_all = _dot(jnp.concatenate(qs, axis=0), kt2)
        ps, invs = [], []
        for g in range(GQ):
            s = jnp.where(mask, s_all[g * BLK:(g + 1) * BLK], NEG_INF)
            sink = sink_ref[hk * GQ + g] * LOG2_E
            m = jnp.maximum(jnp.max(s, axis=-1, keepdims=True), sink)
            p = jnp.exp2(s - m)
            invs.append(1.0 / (jnp.sum(p, axis=-1, keepdims=True) + jnp.exp2(sink - m)))
            ps.append(p.astype(_BF16))
        o_all = _dot(jnp.concatenate(ps, axis=0), v_all)
        for cc in range(GQ // 2):
            even = o_all[(2 * cc) * BLK:(2 * cc + 1) * BLK] * invs[2 * cc]
            odd = o_all[(2 * cc + 1) * BLK:(2 * cc + 2) * BLK] * invs[2 * cc + 1]
            if hk == 0:
                odd = pltpu.roll(odd, HEAD_DIM, 1)
            else:
                even = pltpu.roll(even, HEAD_DIM, 1)
            chunks.append(jnp.where(lane_o < HEAD_DIM, even, odd))
    ssq = sum(jnp.sum(ch * ch, axis=-1, keepdims=True) for ch in chunks)
    scale = lax.rsqrt(ssq * (1.0 / D_ATTN) + LN_EPS)
    return [ch * scale * gn_ref[:, c * LANES:(c + 1) * LANES] for c, ch in enumerate(chunks)]


def _attention(q, kt, v, sink, gn, n_lat, ctx_len):
    nb, r, _ = q.shape
    pair = 2 * BLK
    assert (n_lat * BLK) % pair == 0 and ctx_len % pair == 0
    ctx_blk = (n_lat * BLK) // ctx_len
    n_pairs = n_lat // 2
    i_prev = lambda i: jnp.clip(2 * i - 1, 0, n_lat - 1)
    i_next = lambda i: jnp.clip(2 * i + 2, 0, n_lat - 1)
    i_mid = lambda i: jnp.minimum(i, n_pairs - 1)
    edge = lambda f: pl.BlockSpec((1, BLK, D_KV), lambda b, i: (b, f(i), 0))
    edge_t = lambda f: pl.BlockSpec((1, D_KV, BLK), lambda b, i: (b, 0, f(i)))
    mid = pl.BlockSpec((1, pair, D_KV), lambda b, i: (b, i_mid(i), 0))
    mid_t = pl.BlockSpec((1, D_KV, pair), lambda b, i: (b, 0, i_mid(i)))
    ctx = pl.BlockSpec((1, ctx_len, D_KV), lambda b, i: (b, ctx_blk, 0))
    ctx_t = pl.BlockSpec((1, D_KV, ctx_len), lambda b, i: (b, 0, ctx_blk))
    return pl.pallas_call(
        functools.partial(_attn_kernel, n_lat),
        grid=(nb, r // pair),
        in_specs=[
            pl.BlockSpec(memory_space=pltpu.SMEM),
            pl.BlockSpec((1, pair, D_ATTN), lambda b, i: (b, i, 0)),
            edge_t(i_prev), mid_t, edge_t(i_next), ctx_t, edge(i_prev), mid, edge(i_next), ctx,
            pl.BlockSpec((1, D_ATTN), lambda b, i: (0, 0)),
        ],
        out_specs=pl.BlockSpec((1, pair, D_ATTN), lambda b, i: (b, i, 0)),
        out_shape=jax.ShapeDtypeStruct((nb, r, D_ATTN), _BF16),
        compiler_params=_cparams("arbitrary", "arbitrary"),
        name="attention",
    )(sink, q, kt, kt, kt, kt, v, v, v, v, gn)


def _first_lane(cond, lane):
    return jnp.min(jnp.where(cond, lane, LOGIT_W), axis=-1, keepdims=True)


def _route_tile(logits, counts, lower):
    lane = lax.broadcasted_iota(jnp.int32, logits.shape, 1)
    is_group = lane < N_GROUPS
    g_max = jnp.max(jnp.where(is_group, logits, -jnp.inf), axis=-1, keepdims=True)
    g_sel = _first_lane(is_group & (logits == g_max), lane)
    p_g = 1.0 / jnp.sum(jnp.where(is_group, jnp.exp(logits - g_max), 0.0), axis=-1, keepdims=True)
    e_lo = N_GROUPS + EXP_PER_GROUP * g_sel
    in_grp = (lane >= e_lo) & (lane < e_lo + EXP_PER_GROUP)
    v1 = jnp.max(jnp.where(in_grp, logits, -jnp.inf), axis=-1, keepdims=True)
    i1 = _first_lane(in_grp & (logits == v1), lane)
    rest = in_grp & (lane != i1)
    v2 = jnp.max(jnp.where(rest, logits, -jnp.inf), axis=-1, keepdims=True)
    i2 = _first_lane(rest & (logits == v2), lane)
    e21 = jnp.exp(v2 - v1)
    gate1 = p_g / (1.0 + e21)
    gate2 = gate1 * e21
    eid1 = i1 - N_GROUPS
    eid2 = i2 - N_GROUPS
    oh1 = lane == eid1
    oh2 = lane == eid2
    onehot = jnp.where(oh1 | oh2, 1.0, 0.0)
    before = _dot(lower, onehot.astype(_BF16)) + counts
    rank1 = jnp.sum(jnp.where(oh1, before, 0.0), axis=-1, keepdims=True)
    rank2 = jnp.sum(jnp.where(oh2, before, 0.0), axis=-1, keepdims=True)
    new_counts = counts + jnp.sum(onehot, axis=0, keepdims=True)
    fields = (eid1.astype(_F32), eid2.astype(_F32), rank1, rank2, gate1, gate2)
    slab = jnp.zeros(logits.shape, _F32)
    for i, f in enumerate(fields):
        slab = jnp.where(lane == i, f, slab)
    return slab, new_counts


def _outproj_kernel(alpha, n_batch, rn_ref, an_ref, x_ref, mods_ref, wr_ref, wa_ref, lg_ref, lb_ref,
                    wrt_ref, brt_ref, x1_ref, h2_ref, lo_ref):
    for s in range(x_ref.shape[1] // ROW_TILE):
        rows = slice(s * ROW_TILE, (s + 1) * ROW_TILE)
        m = _sub_tile_mod(s, x_ref.shape[1] // ROW_TILE, n_batch)
        y = _dot(rn_ref[0, rows], wr_ref[...]) + _dot(an_ref[0, rows], wa_ref[...])
        z = alpha * x_ref[0, rows] + mods_ref[m, 2:3, :] * y
        x1 = _normalize(z) * lg_ref[...] + lb_ref[...]
        x1_ref[0, rows] = x1
        h2 = _normalize(x1) * (1.0 + mods_ref[m, 4:5, :]) + mods_ref[m, 3:4, :]
        _store_row_tiles(h2_ref, s * ROW_TILE, h2)
        lo_ref[0, rows] = _dot(h2.astype(_BF16), wrt_ref[...]) + brt_ref[...]


def _outproj(rn, an, x, mods, w_out, ln_g, ln_b, w_rt, b_rt, alpha):
    nb, r, d = x.shape
    tm = ROW_TILE * ROW_GROUP
    row = lambda w: pl.BlockSpec((1, tm, w), lambda b, j: (b, j, 0))
    full = lambda shape: pl.BlockSpec(shape, lambda b, j: (0,) * len(shape))
    return pl.pallas_call(
        functools.partial(_outproj_kernel, alpha, nb),
        grid=(nb, r // tm),
        in_specs=[
            row(D_RNN), row(D_ATTN), row(d), full(mods.shape),
            pl.BlockSpec((D_RNN, d), lambda b, j: (0, 0)),
            pl.BlockSpec((D_ATTN, d), lambda b, j: (1, 0)),
            full((1, d)), full((1, d)), full((d, LOGIT_W)), full((1, LOGIT_W)),
        ],
        out_specs=[row(d), pl.BlockSpec((tm * ROW_TILE_SUB, LANES), lambda b, j: (b * (r // tm) + j, 0)),
                   row(LOGIT_W)],
        out_shape=[
            jax.ShapeDtypeStruct((nb, r, d), _F32),
            jax.ShapeDtypeStruct((nb * r * ROW_TILE_SUB, LANES), _F32),
            jax.ShapeDtypeStruct((nb, r, LOGIT_W), _F32),
        ],
        compiler_params=_cparams("arbitrary", "arbitrary"),
        name="outproj",
    )(rn, an, x, mods, w_out, w_out, ln_g, ln_b, w_rt, b_rt)


def _route_kernel(lg_ref, low_ref, rt_ref, cnt_ref, cnt_s):
    @pl.when(pl.program_id(0) == 0)
    def _():
        cnt_s[...] = jnp.zeros_like(cnt_s)

    slab, counts = _route_tile(lg_ref[...], cnt_s[...], low_ref[...])
    rt_ref[...] = slab
    cnt_s[...] = counts
    cnt_ref[...] = counts


def _route_slab(logits):
    nt = logits.shape[0]
    tm = next(t for t in ROUTE_TILES if nt % t == 0)
    lower = jnp.tril(jnp.ones((tm, tm), _BF16), -1)
    return pl.pallas_call(
        _route_kernel,
        grid=(nt // tm,),
        in_specs=[pl.BlockSpec((tm, LOGIT_W), lambda i: (i, 0)), pl.BlockSpec((tm, tm), lambda i: (0, 0))],
        out_specs=[pl.BlockSpec((tm, LOGIT_W), lambda i: (i, 0)), pl.BlockSpec((1, LOGIT_W), lambda i: (0, 0))],
        out_shape=[jax.ShapeDtypeStruct((nt, LOGIT_W), _F32), jax.ShapeDtypeStruct((1, LOGIT_W), _F32)],
        scratch_shapes=[pltpu.VMEM((1, LOGIT_W), _F32)],
        compiler_params=_cparams("arbitrary"),
        name="route",
    )(logits, lower)


def _tile_rows(i):
    return pl.ds(pl.multiple_of(i * ROW_TILE_SUB, ROW_TILE_SUB), ROW_TILE_SUB)


def _row_gather(idx_ref, n_rows, src_hbm, dst, dst_base, sem):
    def body(k, carry):
        for p in range(2):
            i = 2 * k + p
            t = idx_ref[0, 0, i]
            pltpu.make_async_copy(src_hbm.at[_tile_rows(t), :], dst.at[_tile_rows(dst_base + i), :],
                                  sem).start(priority=p)
        return carry

    lax.fori_loop(0, n_rows // 2, body, 0, unroll=4)


def _gather_wait(n_rows, src_hbm, dst, dst_base, sem):
    pltpu.make_async_copy(src_hbm.at[pl.ds(0, n_rows * ROW_TILE_SUB), :],
                          dst.at[pl.ds(pl.multiple_of(dst_base * ROW_TILE_SUB, ROW_TILE_SUB),
                                       n_rows * ROW_TILE_SUB), :], sem).wait()


def _moe_kernel(blk_e_ref, tok_ref, tok_next_ref, h_hbm, w1_ref, w3_ref, w2_ref, o_ref,
                xbuf, sem, w1_s, w3_s, w2_s):
    j = pl.program_id(0)
    n_used = blk_e_ref[pl.num_programs(0)]
    slot = j % 2
    bm = tok_ref.shape[2]

    @pl.when(j == 0)
    def _():
        _row_gather(tok_ref, bm, h_hbm, xbuf, 0, sem.at[0])

    @pl.when(j + 1 < n_used)
    def _():
        _row_gather(tok_next_ref, bm, h_hbm, xbuf, (1 - slot) * bm, sem.at[1 - slot])

    @pl.when(j < n_used)
    def _():
        @pl.when((j == 0) | (blk_e_ref[j] != blk_e_ref[jnp.maximum(j - 1, 0)]))
        def _():
            w1_s[...] = w1_ref[0, 0].astype(_BF16)
            w3_s[...] = w3_ref[0, 0].astype(_BF16)
            w2_s[...] = w2_ref[0, 0].astype(_BF16)

        _gather_wait(bm, h_hbm, xbuf, slot * bm, sem.at[slot])
        x = _load_row_tiles(xbuf, slot * bm, bm).astype(_BF16)
        a = _dot(x, w1_s[...])
        h = (a * _sigmoid(a)) * _dot(x, w3_s[...])
        _store_row_tiles(o_ref, 0, _dot(h.astype(_BF16), w2_s[...]))

    @pl.when(j >= n_used)
    def _():
        o_ref[...] = jnp.zeros_like(o_ref)


def _moe_ffn(h2, slot_tok, blk_e, w1, w3, w2, layer):
    d = D_MODEL
    bm = MOE_TILE
    n_blk = blk_e.shape[0] - 1
    tok3 = slot_tok.reshape(n_blk, 1, bm)
    smem_blk = lambda f: pl.BlockSpec((1, 1, bm), lambda j, be: (f(j), 0, 0), memory_space=pltpu.SMEM)
    grid_spec = pltpu.PrefetchScalarGridSpec(
        num_scalar_prefetch=1,
        grid=(n_blk,),
        in_specs=[
            smem_blk(lambda j: j),
            smem_blk(lambda j: jnp.minimum(j + 1, n_blk - 1)),
            pl.BlockSpec(memory_space=pl.ANY),
            pl.BlockSpec((1, 1, d, D_EXPERT), lambda j, be: (layer, be[j], 0, 0)),
            pl.BlockSpec((1, 1, d, D_EXPERT), lambda j, be: (layer, be[j], 0, 0)),
            pl.BlockSpec((1, 1, D_EXPERT, d), lambda j, be: (layer, be[j], 0, 0)),
        ],
        out_specs=pl.BlockSpec((bm * ROW_TILE_SUB, LANES), lambda j, be: (j, 0)),
        scratch_shapes=[
            pltpu.VMEM((2 * bm * ROW_TILE_SUB, LANES), _F32), pltpu.SemaphoreType.DMA((2,)),
            pltpu.VMEM((d, D_EXPERT), _BF16), pltpu.VMEM((d, D_EXPERT), _BF16),
            pltpu.VMEM((D_EXPERT, d), _BF16),
        ],
    )
    return pl.pallas_call(
        _moe_kernel,
        grid_spec=grid_spec,
        out_shape=jax.ShapeDtypeStruct((n_blk * bm * ROW_TILE_SUB, LANES), _F32),
        compiler_params=_cparams("arbitrary"),
        name="moe_ffn",
    )(blk_e, tok3, tok3, h2, w1, w3, w2)


def _combine_kernel(alpha, dst_ref, dst_next_ref, y_hbm, gate_ref, x_ref, mod_ref, lg_ref, lb_ref,
                    o_ref, ybuf, sem):
    s = pl.program_id(0)
    n = pl.num_programs(0)
    slot = s % 2
    tm = x_ref.shape[0]

    nr = TOP_K * tm

    @pl.when(s == 0)
    def _():
        _row_gather(dst_ref, nr, y_hbm, ybuf, 0, sem.at[0])

    @pl.when(s + 1 < n)
    def _():
        _row_gather(dst_next_ref, nr, y_hbm, ybuf, (1 - slot) * nr, sem.at[1 - slot])

    _gather_wait(nr, y_hbm, ybuf, slot * nr, sem.at[slot])
    gate = gate_ref[...]
    y = (_load_row_tiles(ybuf, slot * nr, tm) * gate[:, 0:1]
         + _load_row_tiles(ybuf, slot * nr + tm, tm) * gate[:, 1:2])
    z = alpha * x_ref[...] + mod_ref[0, 5:6, :] * y
    o_ref[...] = _normalize(z) * lg_ref[...] + lb_ref[...]


def _combine(yb, dest, gate, x1, mods, ln_g, ln_b, n_batch, tiles_per_batch, tiles_used, alpha):
    nt, d = x1.shape
    tm = ROW_TILE
    n_steps = n_batch * tiles_used
    dst3 = dest.reshape(nt // tm, tm, TOP_K).transpose(0, 2, 1).reshape(nt // tm, 1, TOP_K * tm)

    def tile(s):
        return (s // tiles_used) * tiles_per_batch + s % tiles_used

    def mod_idx(s):
        return (jnp.where(s % tiles_used >= tiles_per_batch - 1, n_batch, s // tiles_used), 0, 0)

    smem_blk = lambda f: pl.BlockSpec((1, 1, TOP_K * tm), lambda s: (tile(f(s)), 0, 0),
                                      memory_space=pltpu.SMEM)
    full = lambda shape: pl.BlockSpec(shape, lambda s: (0,) * len(shape))
    return pl.pallas_call(
        functools.partial(_combine_kernel, alpha),
        grid=(n_steps,),
        in_specs=[
            smem_blk(lambda s: s),
            smem_blk(lambda s: jnp.minimum(s + 1, n_steps - 1)),
            pl.BlockSpec(memory_space=pl.ANY),
            pl.BlockSpec((tm, TOP_K), lambda s: (tile(s), 0)),
            pl.BlockSpec((tm, d), lambda s: (tile(s), 0)),
            pl.BlockSpec((1, 6, d), mod_idx),
            full((1, d)), full((1, d)),
        ],
        out_specs=pl.BlockSpec((tm, d), lambda s: (s, 0)),
        out_shape=jax.ShapeDtypeStruct((n_steps * tm, d), _F32),
        scratch_shapes=[pltpu.VMEM((2 * TOP_K * tm * ROW_TILE_SUB, LANES), _F32),
                        pltpu.SemaphoreType.DMA((2,))],
        compiler_params=_cparams("arbitrary"),
        name="combine_ln2",
    )(dst3, dst3, yb, gate, x1, mods, ln_g, ln_b)


def _route(slab, counts, n_blk):
    nt = slab.shape[0]
    bm = MOE_TILE
    eid = slab[:, 0:TOP_K].astype(jnp.int32)
    rank = slab[:, TOP_K:2 * TOP_K].astype(jnp.int32)
    gate = slab[:, 2 * TOP_K:3 * TOP_K]
    counts = counts[0, :N_EXPERTS].astype(jnp.int32)
    experts = jnp.arange(N_EXPERTS, dtype=jnp.int32)
    pcounts = (counts + bm - 1) // bm * bm
    pends = jnp.cumsum(pcounts)
    pstarts = pends - pcounts
    dest = jnp.sum(jnp.where(eid[:, :, None] == experts, pstarts, 0), axis=-1) + rank
    n_assign = nt * TOP_K
    n_fill = n_blk * bm - n_assign
    pad_ends = jnp.cumsum(pcounts - counts)
    fill_key = jnp.sum(pad_ends[None, :] <= jnp.arange(n_fill, dtype=jnp.int32)[:, None], axis=-1,
                       dtype=jnp.int32)
    keys = jnp.concatenate([eid.reshape(-1), fill_key])
    vals = jnp.concatenate([jnp.arange(n_assign, dtype=jnp.int32) // TOP_K, jnp.zeros((n_fill,), jnp.int32)])
    _, slot_tok = lax.sort_key_val(keys, vals)
    blk_start = jnp.arange(n_blk, dtype=jnp.int32) * bm
    blk_e = jnp.minimum(jnp.sum(pends[None, :] <= blk_start[:, None], axis=-1, dtype=jnp.int32), N_EXPERTS - 1)
    blk_meta = jnp.concatenate([blk_e, pends[-1:] // bm]).astype(jnp.int32)
    return dest, gate, slot_tok, blk_meta


def _rope_tables(seq, ident_rows):
    rows = seq // GRID_W
    row = jnp.repeat(jnp.arange(rows, dtype=_F32), GRID_W)
    col = jnp.tile(jnp.arange(GRID_W, dtype=_F32), rows)
    inv = ROPE_BASE ** (-jnp.arange(0, AXIS_DIM, 2, dtype=_F32) / AXIS_DIM)
    ang_r = row[:, None] * inv
    ang_c = col[:, None] * inv
    cr, sr, cc, sc = jnp.cos(ang_r), jnp.sin(ang_r), jnp.cos(ang_c), jnp.sin(ang_c)
    z = jnp.zeros_like(sr)
    reps = LANES // HEAD_DIM
    cos = jnp.tile(jnp.concatenate([cr, cr, cc, cc], -1), (1, reps))
    sin_a = jnp.tile(jnp.concatenate([-sr, z, -sc, z], -1), (1, reps))
    sin_b = jnp.tile(jnp.concatenate([z, sr, z, sc], -1), (1, reps))
    one = jnp.ones((ident_rows, LANES), _F32)
    zero = jnp.zeros((ident_rows, LANES), _F32)
    return (jnp.concatenate([cos, one], 0), jnp.concatenate([sin_a, zero], 0),
            jnp.concatenate([sin_b, zero], 0))


def _block_diag(w):
    eye = jnp.eye(RG_BLOCKS, dtype=w.dtype)
    return jnp.einsum("hij,hg->higj", w, eye).reshape(D_RNN, D_RNN)


def kernel(x, c, ctx, c_ctx, w_mod, b_mod, w_in, conv_w, conv_b, rg_wa, rg_ba, rg_wx, rg_bx, rg_lam,
           attn_sink, gn_rnn, gn_attn, w_out, ln1_g, ln1_b, router_g, router_gb, router_e, router_eb,
           w1, w3, w2, ln2_g, ln2_b):
    nb, seq, d = x.shape
    ctx_len = ctx.shape[1]
    depth = w_mod.shape[0]
    r = seq + ctx_len
    nt = nb * r
    alpha = (2 * depth) ** 0.25
    assert d == D_MODEL and seq % ROW_TILE == 0 and ctx_len % ROW_TILE == 0
    assert seq % GRID_W == 0 and seq % ctx_len == 0 and ctx_len % BLK == 0
    n_lat_tiles = seq // ROW_TILE
    tiles_per_batch = r // ROW_TILE
    assert tiles_per_batch == n_lat_tiles + 1, "one context tile per batch"
    assert tiles_per_batch % ROW_GROUP == 0
    n_blk = (nt * TOP_K + N_EXPERTS * (MOE_TILE - 1) + MOE_TILE - 1) // MOE_TILE

    pad_rows = (-(nb + 1)) % SUBLANES
    c_all = jnp.concatenate([c, c_ctx[None, :], jnp.zeros((pad_rows, d), _F32)], 0)
    mods_all = _modulation(c_all, w_mod, b_mod).reshape(depth, nb + 1 + pad_rows, 6, d)

    rope = _rope_tables(seq, ROW_TILE)
    xs = jnp.concatenate([x, ctx], axis=1)

    out = None
    for l in range(depth):
        last = l == depth - 1
        mods = mods_all[l]
        u, g, q, k, v = _inproj(xs, mods, w_in[l].astype(_BF16), rope)

        n_lat_s, n_ctx_s = seq // SCAN_TILE, ctx_len // SCAN_TILE
        gate_w = [(0.5 * jnp.concatenate([_block_diag(rg_wa[l, dr]), _block_diag(rg_wx[l, dr])], 1)).astype(_BF16)
                  for dr in range(2)]
        gate_b = [0.5 * jnp.concatenate([rg_ba[l, dr], rg_bx[l, dr]])[None, :] for dr in range(2)]
        xc, hf = _rglru_fwd(u, conv_w[l], conv_b[l][None, :], gate_w[0], gate_b[0], rg_lam[l, 0][None, :],
                            n_lat_s, n_ctx_s)
        rn = _rglru_bwd(xc, gate_w[1], gate_b[1], rg_lam[l, 1][None, :], hf, g, gn_rnn[l][None, :],
                        n_lat_s, n_ctx_s)

        an = _attention(q, k, v, attn_sink[l], gn_attn[l][None, :], seq // BLK, ctx_len)

        w_rt = jnp.concatenate([router_g[l], router_e[l],
                                jnp.zeros((d, LOGIT_W - N_GROUPS - N_EXPERTS), _F32)], 1).astype(_BF16)
        b_rt = jnp.concatenate([router_gb[l], router_eb[l],
                                jnp.zeros((LOGIT_W - N_GROUPS - N_EXPERTS,), _F32)])[None, :]
        x1, h2, logits = _outproj(rn, an, xs, mods, w_out[l].astype(_BF16), ln1_g[l][None, :],
                                  ln1_b[l][None, :], w_rt, b_rt, alpha)
        slab, counts = _route_slab(logits.reshape(nt, LOGIT_W))
        dest, gate, slot_tok, blk_e = _route(slab, counts, n_blk)
        yb = _moe_ffn(h2, slot_tok, blk_e, w1, w3, w2, l)
        tiles_used = n_lat_tiles if last else tiles_per_batch
        x2 = _combine(yb, dest, gate, x1.reshape(nt, d), mods, ln2_g[l][None, :], ln2_b[l][None, :],
                      nb, tiles_per_batch, tiles_used, alpha)
        if last:
            out = x2.reshape(nb, seq, d)
        else:
            xs = x2.reshape(nb, r, d)
    return out
```
